```python
import jax, jax.numpy as jnp
from jax import lax
import numpy as np

D_MODEL = 2048
BATCH = 8
SEQ = 2048
DEPTH = 2

CTX_LEN = 256
GRID_W = 64
NORM_EPS = 1e-6
SSD_WIDTH = D_MODEL // 2
SSD_HEAD_DIM = 64
SSD_HEADS = SSD_WIDTH // SSD_HEAD_DIM
SSD_GROUPS = 2
SSD_STATE = 128
SSD_CONV_W = 5
SSD_CHUNK = 128
SSD_GN = SSD_GROUPS * SSD_STATE
SSD_XBC = SSD_WIDTH + 2 * SSD_GN
SSD_COLS = SSD_WIDTH + SSD_XBC + 2 * SSD_HEADS
SC_WIDTH = D_MODEL - SSD_WIDTH
SC_CONV_W = 3
SC_COLS = 3 * SC_WIDTH
IN_COLS = SSD_COLS + SC_COLS
MIX_WIDTH = SSD_WIDTH + SC_WIDTH
FFN_HIDDEN = ((8 * D_MODEL + 3 * 256 - 1) // (3 * 256)) * 256

kernel_name = "hybrid_ssd_shortconv_prefix_dit"


def rmsnorm(x, g):
    xf = x.astype(jnp.float32)
    y = xf * lax.rsqrt(jnp.mean(xf * xf, axis=-1, keepdims=True) + NORM_EPS)
    return (y * g.astype(jnp.float32)).astype(x.dtype)


def modulate(h, shift, scale):
    return h * (1.0 + scale[:, None, :]) + shift[:, None, :]


def dwconv(u, w):
    k = w.shape[0]
    return lax.conv_general_dilated(u, w.astype(u.dtype)[:, None, :], window_strides=(1,),
                                    padding=[(k // 2, k // 2)],
                                    dimension_numbers=("NWC", "WIO", "NWC"),
                                    feature_group_count=u.shape[-1])


def flip_seq(t):
    return jnp.flip(t, axis=1)


def to_col_major(t, rows):
    b, L, ch = t.shape
    return t.reshape(b, rows, GRID_W, ch).transpose(0, 2, 1, 3).reshape(b, L, ch)


def from_col_major(t, rows):
    b, L, ch = t.shape
    return t.reshape(b, GRID_W, rows, ch).transpose(0, 2, 1, 3).reshape(b, L, ch)


def zero_state(b):
    return jnp.zeros((b, SSD_HEADS, SSD_HEAD_DIM, SSD_STATE), jnp.float32)


def ssd_scan(xs, dt, a, bm, cm, h0, return_y=True):
    b, L, H, P = xs.shape
    G, N = bm.shape[-2], bm.shape[-1]
    R = H // G
    nc = L // SSD_CHUNK
    X = (xs * dt[..., None]).reshape(b, nc, SSD_CHUNK, G, R, P)
    a_cs = jnp.cumsum((dt * a).reshape(b, nc, SSD_CHUNK, G, R), axis=2)
    Bc = bm.reshape(b, nc, SSD_CHUNK, G, N)
    decay_to_end = jnp.exp(a_cs[:, :, -1:] - a_cs)
    states = jnp.einsum("bclgn,bclgr,bclgrp->bcgrpn", Bc, decay_to_end, X)
    chunk_decay = jnp.exp(a_cs[:, :, -1])

    def step(h, inp):
        st, dec = inp
        return dec[..., None, None] * h + st, h

    h_last, h_enter = lax.scan(step, h0.reshape(b, G, R, P, N),
                               (jnp.moveaxis(states, 1, 0), jnp.moveaxis(chunk_decay, 1, 0)))
    h_last = h_last.reshape(b, H, P, N)
    if not return_y:
        return h_last
    h_enter = jnp.moveaxis(h_enter, 0, 1)
    Cc = cm.reshape(b, nc, SSD_CHUNK, G, N)
    lower = jnp.tril(jnp.ones((SSD_CHUNK, SSD_CHUNK), dtype=bool))
    seg = a_cs[:, :, :, None] - a_cs[:, :, None, :]
    decay_ls = jnp.exp(jnp.where(lower[:, :, None, None], seg, -jnp.inf))
    cb = jnp.einsum("bclgn,bcsgn->bclsg", Cc, Bc)
    y_diag = jnp.einsum("bclsgr,bcsgrp->bclgrp", cb[..., None] * decay_ls, X)
    y_off = jnp.einsum("bclgn,bcgrpn,bclgr->bclgrp", Cc, h_enter, jnp.exp(a_cs))
    return (y_diag + y_off).reshape(b, L, H, P), h_last


def ssd_inputs(p_ssd, conv_w, conv_b, dt_bias, a_log):
    b, L, _ = p_ssd.shape
    f32 = jnp.float32
    xbc = jax.nn.silu(dwconv(p_ssd[..., SSD_WIDTH:SSD_WIDTH + SSD_XBC], conv_w) + conv_b.astype(p_ssd.dtype))
    xs = xbc[..., :SSD_WIDTH].astype(f32).reshape(b, L, SSD_HEADS, SSD_HEAD_DIM)
    bm = xbc[..., SSD_WIDTH:SSD_WIDTH + SSD_GN].astype(f32).reshape(b, L, SSD_GROUPS, SSD_STATE)
    cm = xbc[..., SSD_WIDTH + SSD_GN:].astype(f32).reshape(b, L, SSD_GROUPS, SSD_STATE)
    dt_raw = p_ssd[..., SSD_WIDTH + SSD_XBC:SSD_COLS].astype(f32).reshape(b, L, 2, SSD_HEADS)
    dt = jax.nn.softplus(dt_raw + dt_bias.astype(f32))
    a = -jnp.exp(a_log.astype(f32))
    return xs, bm, cm, dt, a


def bidir_ssd(xs, bm, cm, dt, a, h0_f, h0_b, return_y=True):
    fwd = ssd_scan(xs, dt[:, :, 0], a[0], bm, cm, h0_f, return_y)
    bwd = ssd_scan(flip_seq(xs), flip_seq(dt[:, :, 1]), a[1], flip_seq(bm), flip_seq(cm), h0_b, return_y)
    if not return_y:
        return fwd, bwd
    (y_f, h_f), (y_b, h_b) = fwd, bwd
    return y_f + flip_seq(y_b), h_f, h_b


def token_mixers(proj, conv_w, conv_b, dt_bias, a_log, d_skip, ssd_g, sc_w, h0_f, h0_b):
    b, L, _ = proj.shape
    z = proj[..., :SSD_WIDTH]
    xs, bm, cm, dt, a = ssd_inputs(proj, conv_w, conv_b, dt_bias, a_log)
    y, h_f, h_b = bidir_ssd(xs, bm, cm, dt, a, h0_f, h0_b)
    y = (y + d_skip.astype(jnp.float32)[:, None] * xs).reshape(b, L, SSD_WIDTH).astype(proj.dtype)
    y_ssd = rmsnorm(y * jax.nn.silu(z), ssd_g)
    gate_b, gate_c, val = jnp.split(proj[..., SSD_COLS:], 3, axis=-1)
    y_sc = gate_b * dwconv(gate_c * val, sc_w)
    return jnp.concatenate([y_ssd, y_sc], axis=-1), h_f, h_b


def swiglu(h, w_gate, w_up, w_down):
    return (jax.nn.silu(h @ w_gate) * (h @ w_up)) @ w_down


def setup_inputs(seed: int = 0) -> dict:
    key = jax.random.key(seed)
    ks = jax.random.split(key, 24)
    f32 = jnp.float32
    nrm = lambda k, shape, s: jax.random.normal(k, shape, f32) * s
    dt0 = jnp.exp(jax.random.uniform(ks[10], (DEPTH, 2, SSD_HEADS), f32, np.log(1e-3), np.log(1e-1)))
    return {
        "x": nrm(ks[0], (BATCH, SEQ, D_MODEL), 1.0),
        "c": nrm(ks[1], (BATCH, D_MODEL), 1.0),
        "ctx": nrm(ks[2], (BATCH, CTX_LEN, D_MODEL), 1.0),
        "c_ctx": nrm(ks[3], (D_MODEL,), 1.0),
        "ada_w": nrm(ks[4], (DEPTH, D_MODEL, 6 * D_MODEL), 0.5 * D_MODEL ** -0.5),
        "ada_b": nrm(ks[5], (DEPTH, 6 * D_MODEL), 0.01),
        "mix_norm_g": 1.0 + nrm(ks[6], (DEPTH, D_MODEL), 0.05),
        "w_in": nrm(ks[7], (DEPTH, D_MODEL, IN_COLS), D_MODEL ** -0.5),
        "ssd_conv_w": nrm(ks[8], (DEPTH, SSD_CONV_W, SSD_XBC), SSD_CONV_W ** -0.5),
        "ssd_conv_b": nrm(ks[9], (DEPTH, SSD_XBC), 0.01),
        "ssd_dt_bias": dt0 + jnp.log(-jnp.expm1(-dt0)),
        "ssd_a_log": jnp.log(jax.random.uniform(ks[11], (DEPTH, 2, SSD_HEADS), f32, 1.0, 16.0)),
        "ssd_d": 1.0 + nrm(ks[12], (DEPTH, SSD_HEADS), 0.05),
        "ssd_norm_g": 1.0 + nrm(ks[13], (DEPTH, SSD_WIDTH), 0.05),
        "sc_conv_w": nrm(ks[14], (DEPTH, SC_CONV_W, SC_WIDTH), SC_CONV_W ** -0.5),
        "w_out": nrm(ks[15], (DEPTH, MIX_WIDTH, D_MODEL), MIX_WIDTH ** -0.5),
        "ffn_norm_g": 1.0 + nrm(ks[16], (DEPTH, D_MODEL), 0.05),
        "w_gate": nrm(ks[17], (DEPTH, D_MODEL, FFN_HIDDEN), D_MODEL ** -0.5),
        "w_up": nrm(ks[18], (DEPTH, D_MODEL, FFN_HIDDEN), D_MODEL ** -0.5),
        "w_down": nrm(ks[19], (DEPTH, FFN_HIDDEN, D_MODEL), FFN_HIDDEN ** -0.5),
        "final_norm_g": 1.0 + nrm(ks[20], (D_MODEL,), 0.05),
    }


def reference(x, c, ctx, c_ctx, ada_w, ada_b, mix_norm_g, w_in, ssd_conv_w, ssd_conv_b, ssd_dt_bias,
              ssd_a_log, ssd_d, ssd_norm_g, sc_conv_w, w_out, ffn_norm_g, w_gate, w_up, w_down,
              final_norm_g):
    rows = x.shape[1] // GRID_W
    h_ctx = ctx
    for l in range(DEPTH):
        last = l == DEPTH - 1
        mx = jnp.split(jax.nn.silu(c) @ ada_w[l] + ada_b[l], 6, axis=-1)
        mc = jnp.split(jax.nn.silu(c_ctx)[None] @ ada_w[l] + ada_b[l], 6, axis=-1)
        layer_mix = (ssd_conv_w[l], ssd_conv_b[l], ssd_dt_bias[l], ssd_a_log[l], ssd_d[l],
                     ssd_norm_g[l], sc_conv_w[l])

        hc = modulate(rmsnorm(h_ctx, mix_norm_g[l]), mc[0], mc[1])
        if last:
            proj_c = hc @ w_in[l][:, :SSD_COLS]
            xs_c, bm_c, cm_c, dt_c, a_c = ssd_inputs(proj_c, ssd_conv_w[l], ssd_conv_b[l],
                                                     ssd_dt_bias[l], ssd_a_log[l])
            state_f, state_b = bidir_ssd(xs_c, bm_c, cm_c, dt_c, a_c, zero_state(hc.shape[0]),
                                         zero_state(hc.shape[0]), return_y=False)
        else:
            mix_c, state_f, state_b = token_mixers(hc @ w_in[l], *layer_mix,
                                                   zero_state(hc.shape[0]), zero_state(hc.shape[0]))
            h_ctx = h_ctx + mc[2][:, None, :] * (mix_c @ w_out[l])
            hf = modulate(rmsnorm(h_ctx, ffn_norm_g[l]), mc[3], mc[4])
            h_ctx = h_ctx + mc[5][:, None, :] * swiglu(hf, w_gate[l], w_up[l], w_down[l])

        hx = modulate(rmsnorm(x, mix_norm_g[l]), mx[0], mx[1])
        col_major = l % 2 == 1
        if col_major:
            hx = to_col_major(hx, rows)
        mix_x, _, _ = token_mixers(hx @ w_in[l], *layer_mix, state_f, state_b)
        if col_major:
            mix_x = from_col_major(mix_x, rows)
        x = x + mx[2][:, None, :] * (mix_x @ w_out[l])
        hf = modulate(rmsnorm(x, ffn_norm_g[l]), mx[3], mx[4])
        x = x + mx[5][:, None, :] * swiglu(hf, w_gate[l], w_up[l], w_down[l])
    return rmsnorm(x, final_norm_g)
```

```python
import functools

import jax
import jax.numpy as jnp
from jax import lax
from jax.experimental import pallas as pl
from jax.experimental.pallas import tpu as pltpu

F32 = jnp.float32
BF16 = jnp.bfloat16

D_MODEL = 2048
GRID_W = 64
NORM_EPS = 1e-6
SSD_WIDTH = 1024
SSD_HEAD_DIM = 64
SSD_HEADS = 16
SSD_GROUPS = 2
SSD_STATE = 128
SSD_CONV_W = 5
SSD_CHUNK = 128
SSD_GN = SSD_GROUPS * SSD_STATE
SSD_XBC = SSD_WIDTH + 2 * SSD_GN
SSD_COLS = SSD_WIDTH + SSD_XBC + 2 * SSD_HEADS
SC_WIDTH = 1024
SC_CONV_W = 3
MAIN_COLS = SSD_WIDTH + SSD_XBC + 3 * SC_WIDTH
FFN_HIDDEN = 5632
LANES = 128
HEAD_GROUP_COLS = SSD_WIDTH // SSD_GROUPS
VMEM_LIMIT = 56 * 1024 * 1024


def _params(*sem):
    return pltpu.CompilerParams(dimension_semantics=sem, vmem_limit_bytes=VMEM_LIMIT)


def _silu(v):
    return v * jax.nn.sigmoid(v)


def _softplus(v):
    return jnp.maximum(v, 0.0) + jnp.log1p(jnp.exp(-jnp.abs(v)))


def _norm_rows(v, g):
    ms = jnp.mean(v * v, axis=-1, keepdims=True)
    return v * lax.rsqrt(ms + NORM_EPS) * g


def _dot(a, b):
    return jnp.dot(a, b, preferred_element_type=F32)


def _adaln_kernel(c_ref, w_ref, b_ref, o_ref):
    s = _silu(c_ref[...]).astype(BF16)
    o_ref[0] = _dot(s, w_ref[0].astype(BF16)) + b_ref[0]


def adaln(cc, ada_w, ada_b, tn=1536):
    depth, d, n = ada_w.shape
    rows = cc.shape[0]
    return pl.pallas_call(
        _adaln_kernel,
        grid=(depth, n // tn),
        in_specs=[
            pl.BlockSpec((rows, d), lambda l, j: (0, 0)),
            pl.BlockSpec((1, d, tn), lambda l, j: (l, 0, j)),
            pl.BlockSpec((1, 1, tn), lambda l, j: (l, 0, j)),
        ],
        out_specs=pl.BlockSpec((1, rows, tn), lambda l, j: (l, 0, j)),
        out_shape=jax.ShapeDtypeStruct((depth, rows, n), F32),
        compiler_params=_params("arbitrary", "arbitrary"),
        name="adaln",
    )(cc, ada_w, ada_b.reshape(depth, 1, n))


def _norm_mod_kernel(x_ref, g_ref, sh_ref, sc_ref, o_ref):
    y = _norm_rows(x_ref[0], g_ref[...])
    o_ref[0] = (y * (1.0 + sc_ref[0]) + sh_ref[0]).astype(o_ref.dtype)


def norm_mod(x, g, shift, scale, tr=512):
    nb, length, d = x.shape
    return pl.pallas_call(
        _norm_mod_kernel,
        grid=(nb, length // tr),
        in_specs=[
            pl.BlockSpec((1, tr, d), lambda b, i: (b, i, 0)),
            pl.BlockSpec((1, d), lambda b, i: (0, 0)),
            pl.BlockSpec((1, 1, d), lambda b, i: (b, 0, 0)),
            pl.BlockSpec((1, 1, d), lambda b, i: (b, 0, 0)),
        ],
        out_specs=pl.BlockSpec((1, tr, d), lambda b, i: (b, i, 0)),
        out_shape=jax.ShapeDtypeStruct((nb, length, d), BF16),
        compiler_params=_params("arbitrary", "arbitrary"),
        name="norm_mod",
    )(x, g.reshape(1, d), shift, scale)


def _matmul_kernel(a_ref, w_ref, o_ref, *, sub):
    a = a_ref[0]
    tn = o_ref.shape[-1]
    for n0 in range(0, tn, sub):
        o_ref[0, :, n0:n0 + sub] = _dot(a, w_ref[:, n0:n0 + sub]).astype(o_ref.dtype)


def matmul(a, w, out_dtype, tm, tn, sub):
    nb, length, k = a.shape
    n = w.shape[1]
    return pl.pallas_call(
        functools.partial(_matmul_kernel, sub=sub),
        grid=(n // tn, nb, length // tm),
        in_specs=[
            pl.BlockSpec((1, tm, k), lambda j, b, i: (b, i, 0)),
            pl.BlockSpec((k, tn), lambda j, b, i: (0, j)),
        ],
        out_specs=pl.BlockSpec((1, tm, tn), lambda j, b, i: (b, i, j)),
        out_shape=jax.ShapeDtypeStruct((nb, length, n), out_dtype),
        compiler_params=_params("arbitrary", "arbitrary", "arbitrary"),
        name="matmul",
    )(a, w)


def _shifted_taps(pad_ref, r0, rows, taps):
    half = taps // 2
    win = pad_ref[r0:r0 + rows + 32, :]
    out = []
    for k in range(taps):
        shift = (half - k) % (rows + 32)
        rolled = win if shift == 0 else pltpu.roll(win, shift, axis=0)
        out.append(rolled[16:16 + rows, :])
    return out


def _fill_padded(pad_ref, vals):
    t = vals.shape[0]
    zeros = jnp.zeros((16, vals.shape[1]), F32)
    pad_ref[0:16, :] = zeros
    pad_ref[t + 16:t + 32, :] = zeros
    pad_ref[16:t + 16, :] = vals


def _conv_xbc_kernel(x_ref, w_ref, b_ref, o_ref, pad_ref, *, rows):
    t = x_ref.shape[1]
    _fill_padded(pad_ref, x_ref[0].astype(F32))
    for r0 in range(0, t, rows):
        taps = _shifted_taps(pad_ref, r0, rows, SSD_CONV_W)
        acc = b_ref[...] + w_ref[0:1, :] * taps[0]
        for k in range(1, SSD_CONV_W):
            acc = acc + w_ref[k:k + 1, :] * taps[k]
        o_ref[0, r0:r0 + rows, :] = _silu(acc).astype(o_ref.dtype)


def conv_xbc(p, conv_w, conv_b, ct=256):
    nseq, t, _ = p.shape
    off = SSD_WIDTH // ct
    w8 = jnp.zeros((8, SSD_XBC), F32).at[:SSD_CONV_W].set(conv_w)
    return pl.pallas_call(
        functools.partial(_conv_xbc_kernel, rows=min(t, 256)),
        grid=(nseq, SSD_XBC // ct),
        in_specs=[
            pl.BlockSpec((1, t, ct), lambda s, j: (s, 0, off + j)),
            pl.BlockSpec((8, ct), lambda s, j: (0, j)),
            pl.BlockSpec((1, ct), lambda s, j: (0, j)),
        ],
        out_specs=pl.BlockSpec((1, t, ct), lambda s, j: (s, 0, j)),
        out_shape=jax.ShapeDtypeStruct((nseq, t, SSD_XBC), BF16),
        scratch_shapes=[pltpu.VMEM((t + 32, ct), F32)],
        compiler_params=_params("arbitrary", "arbitrary"),
        name="conv_xbc",
    )(p, w8, conv_b.reshape(1, SSD_XBC))


def _conv_sc_kernel(gb_ref, gc_ref, v_ref, w_ref, o_ref, pad_ref, *, rows):
    t = v_ref.shape[1]
    _fill_padded(pad_ref, gc_ref[0].astype(F32) * v_ref[0].astype(F32))
    for r0 in range(0, t, rows):
        taps = _shifted_taps(pad_ref, r0, rows, SC_CONV_W)
        acc = w_ref[0:1, :] * taps[0]
        for k in range(1, SC_CONV_W):
            acc = acc + w_ref[k:k + 1, :] * taps[k]
        gate = gb_ref[0, r0:r0 + rows, :].astype(F32)
        o_ref[0, r0:r0 + rows, :] = (gate * acc).astype(o_ref.dtype)


def conv_sc(p, sc_w, ct=256):
    nseq, t, _ = p.shape
    base = (SSD_WIDTH + SSD_XBC) // ct
    step = SC_WIDTH // ct
    w8 = jnp.zeros((8, SC_WIDTH), F32).at[:SC_CONV_W].set(sc_w)
    col = lambda k: pl.BlockSpec((1, t, ct), lambda s, j: (s, 0, base + k * step + j))
    return pl.pallas_call(
        functools.partial(_conv_sc_kernel, rows=min(t, 256)),
        grid=(nseq, SC_WIDTH // ct),
        in_specs=[col(0), col(1), col(2), pl.BlockSpec((8, ct), lambda s, j: (0, j))],
        out_specs=pl.BlockSpec((1, t, ct), lambda s, j: (s, 0, j)),
        out_shape=jax.ShapeDtypeStruct((nseq, t, SC_WIDTH), BF16),
        scratch_shapes=[pltpu.VMEM((t + 32, ct), F32)],
        compiler_params=_params("arbitrary", "arbitrary"),
        name="conv_sc",
    )(p, p, p, w8)


def _cumsum_rows(v, reverse):
    n = v.shape[0]
    row = lax.broadcasted_iota(jnp.int32, v.shape, 0)
    d = 1
    while d < n:
        if reverse:
            v = v + jnp.where(row < n - d, pltpu.roll(v, n - d, axis=0), 0.0)
        else:
            v = v + jnp.where(row >= d, pltpu.roll(v, d, axis=0), 0.0)
        d *= 2
    return v


def _expand_heads(q, off):
    r = q.shape[0]
    lane = lax.broadcasted_iota(jnp.int32, (r, LANES), 1)
    pieces = []
    for i in range(SSD_HEADS // 2):
        lo = jnp.broadcast_to(q[:, off + 2 * i:off + 2 * i + 1], (r, LANES))
        hi = jnp.broadcast_to(q[:, off + 2 * i + 1:off + 2 * i + 2], (r, LANES))
        pieces.append(jnp.where(lane < SSD_HEAD_DIM, lo, hi))
    return jnp.concatenate(pieces, axis=1)


def _state_update(st_ref, b_mat, xw, decay_row):
    for g in range(SSD_GROUPS):
        cols = slice(g * HEAD_GROUP_COLS, (g + 1) * HEAD_GROUP_COLS)
        contrib = lax.dot_general(
            b_mat[:, g * SSD_STATE:(g + 1) * SSD_STATE], xw[:, cols],
            (((0,), (0,)), ((), ())), preferred_element_type=F32)
        st_ref[:, cols] = st_ref[:, cols] * decay_row[:, cols] + contrib


def _ssd_kernel(xbc_ref, dt_ref, z_ref, h0f_ref, h0b_ref, dtb_ref, alog_ref, dsk_ref, g_ref,
                y_ref, hfl_ref, hbl_ref, hbe_ref, stf_ref, stb_ref, *, nc):
    q = SSD_CHUNK
    j = pl.program_id(1)
    fwd_lanes = 0
    bwd_lanes = SSD_HEADS

    dt = _softplus(dt_ref[0] + dtb_ref[...])
    d_a = dt * (-jnp.exp(alog_ref[...]))
    xs_bf = xbc_ref[0, :, 0:SSD_WIDTH]
    xs = xs_bf.astype(F32)
    b_mat = xbc_ref[0, :, SSD_WIDTH:SSD_WIDTH + SSD_GN]

    @pl.when(j == 0)
    def _():
        stf_ref[...] = h0f_ref[0]
        stb_ref[...] = h0b_ref[0]

    @pl.when(j < nc)
    def _backward_states():
        c = nc - 1 - j
        acs_b = _cumsum_rows(d_a, reverse=True)
        tot = acs_b[0:1, :]
        w_full = _expand_heads(dt * jnp.exp(tot - acs_b), bwd_lanes)
        xw = (xs * w_full).astype(BF16)
        decay = _expand_heads(jnp.exp(jnp.broadcast_to(tot, (8, LANES))), bwd_lanes)[0:1, :]
        hbe_ref[c] = stb_ref[...].astype(BF16)
        _state_update(stb_ref, b_mat, xw, decay)

        @pl.when(j == nc - 1)
        def _():
            hbl_ref[0] = stb_ref[...]

    @pl.when(j >= nc)
    def _forward_outputs():
        c = j - nc
        row = lax.broadcasted_iota(jnp.int32, (q, q), 0)
        lane = lax.broadcasted_iota(jnp.int32, (q, q), 1)
        acs_f = _cumsum_rows(d_a, reverse=False)
        acs_b = _cumsum_rows(d_a, reverse=True)
        acs = jnp.where(lane < bwd_lanes, acs_f, acs_b)
        acs_t = acs.T
        dt_t = dt.T
        c_mat = xbc_ref[0, :, SSD_WIDTH + SSD_GN:SSD_XBC]
        causal = row >= lane
        anti = row <= lane

        cb = []
        for g in range(SSD_GROUPS):
            gs = slice(g * SSD_STATE, (g + 1) * SSD_STATE)
            cb.append(lax.dot_general(c_mat[:, gs], b_mat[:, gs], (((1,), (1,)), ((), ())),
                                      preferred_element_type=F32))

        y_parts = []
        for i in range(SSD_HEADS // 2):
            mats = []
            for h in (2 * i, 2 * i + 1):
                hf, hb = fwd_lanes + h, bwd_lanes + h
                seg_f = acs[:, hf:hf + 1] - acs_t[hf:hf + 1, :]
                seg_b = acs[:, hb:hb + 1] - acs_t[hb:hb + 1, :]
                l_f = jnp.exp(jnp.where(causal, seg_f, -jnp.inf)) * dt_t[hf:hf + 1, :]
                l_b = jnp.exp(jnp.where(anti, seg_b, -jnp.inf)) * dt_t[hb:hb + 1, :]
                mats.append((cb[h // (SSD_HEADS // SSD_GROUPS)] * (l_f + l_b)).astype(BF16))
            pair = xs[:, i * LANES:(i + 1) * LANES]
            top = jnp.where(lane < SSD_HEAD_DIM, pair, 0.0).astype(BF16)
            bot = jnp.where(lane >= SSD_HEAD_DIM, pair, 0.0).astype(BF16)
            y_parts.append(_dot(jnp.concatenate(mats, axis=1), jnp.concatenate([top, bot], axis=0)))
        y = jnp.concatenate(y_parts, axis=1)

        e_acs = jnp.exp(acs)
        e_f = _expand_heads(e_acs, fwd_lanes)
        e_b = _expand_heads(e_acs, bwd_lanes)
        st_f = stf_ref[...].astype(BF16)
        st_b = hbe_ref[c]
        off_f, off_b = [], []
        for g in range(SSD_GROUPS):
            gs = slice(g * SSD_STATE, (g + 1) * SSD_STATE)
            cols = slice(g * HEAD_GROUP_COLS, (g + 1) * HEAD_GROUP_COLS)
            off_f.append(_dot(c_mat[:, gs], st_f[:, cols]))
            off_b.append(_dot(c_mat[:, gs], st_b[:, cols]))
        y = y + e_f * jnp.concatenate(off_f, axis=1) + e_b * jnp.concatenate(off_b, axis=1)
        y = y + dsk_ref[...] * xs

        tot = acs_f[q - 1:q, :]
        w_full = _expand_heads(dt * jnp.exp(tot - acs_f), fwd_lanes)
        _state_update(stf_ref, b_mat, (xs * w_full).astype(BF16), e_f[q - 1:q, :])

        z = z_ref[0].astype(F32)
        y_ref[0] = _norm_rows(y * _silu(z), g_ref[...]).astype(y_ref.dtype)

        @pl.when(j == 2 * nc - 1)
        def _():
            hfl_ref[0] = stf_ref[...]


def ssd(xbc, dt_raw, p, h0_f, h0_b, dt_bias, a_log, d_skip, norm_g):
    nseq, t, _ = xbc.shape
    nc = t // SSD_CHUNK
    pad_row = lambda v: jnp.zeros((1, LANES), F32).at[0, :2 * SSD_HEADS].set(v.reshape(-1))
    chunk = lambda s, j: (s, jnp.where(j < nc, nc - 1 - j, j - nc), 0)
    out_chunk = lambda s, j: (s, jnp.maximum(j - nc, 0), 0)
    const2 = lambda s, j: (0, 0)
    seq = lambda s, j: (s, 0, 0)
    state_shape = jax.ShapeDtypeStruct((nseq, SSD_STATE, SSD_WIDTH), F32)
    return pl.pallas_call(
        functools.partial(_ssd_kernel, nc=nc),
        grid=(nseq, 2 * nc),
        in_specs=[
            pl.BlockSpec((1, SSD_CHUNK, SSD_XBC), chunk),
            pl.BlockSpec((1, SSD_CHUNK, LANES), chunk),
            pl.BlockSpec((1, SSD_CHUNK, SSD_WIDTH), out_chunk),
            pl.BlockSpec((1, SSD_STATE, SSD_WIDTH), seq),
            pl.BlockSpec((1, SSD_STATE, SSD_WIDTH), seq),
            pl.BlockSpec((1, LANES), const2),
            pl.BlockSpec((1, LANES), const2),
            pl.BlockSpec((1, SSD_WIDTH), const2),
            pl.BlockSpec((1, SSD_WIDTH), const2),
        ],
        out_specs=[
            pl.BlockSpec((1, SSD_CHUNK, SSD_WIDTH), out_chunk),
            pl.BlockSpec((1, SSD_STATE, SSD_WIDTH), seq),
            pl.BlockSpec((1, SSD_STATE, SSD_WIDTH), seq),
        ],
        out_shape=[jax.ShapeDtypeStruct((nseq, t, SSD_WIDTH), BF16), state_shape, state_shape],
        scratch_shapes=[
            pltpu.VMEM((nc, SSD_STATE, SSD_WIDTH), BF16),
            pltpu.VMEM((SSD_STATE, SSD_WIDTH), F32),
            pltpu.VMEM((SSD_STATE, SSD_WIDTH), F32),
        ],
        compiler_params=_params("arbitrary", "arbitrary"),
        name="ssd",
    )(xbc, dt_raw, p, h0_f, h0_b, pad_row(dt_bias), pad_row(a_log),
      jnp.repeat(d_skip, SSD_HEAD_DIM).reshape(1, SSD_WIDTH), norm_g.reshape(1, SSD_WIDTH))


def _out_proj_kernel(y1_ref, y2_ref, w1_ref, w2_ref, x_ref, gate_ref, g_ref, sh_ref, sc_ref,
                     xo_ref, h_ref, *, sub):
    y1 = y1_ref[0]
    y2 = y2_ref[0]
    n = xo_ref.shape[-1]
    for n0 in range(0, n, sub):
        ns = slice(n0, n0 + sub)
        acc = _dot(y1, w1_ref[:, ns]) + _dot(y2, w2_ref[:, ns])
        xo_ref[0, :, ns] = x_ref[0, :, ns] + gate_ref[0, :, ns] * acc
    hn = _norm_rows(xo_ref[0], g_ref[...])
    h_ref[0] = (hn * (1.0 + sc_ref[0]) + sh_ref[0]).astype(h_ref.dtype)


def out_proj(y_ssd, y_sc, w1, w2, x, gate, g, shift, scale, tm=512, sub=1024):
    nb, length, d = x.shape
    k1, k2 = y_ssd.shape[-1], y_sc.shape[-1]
    tok = lambda b, i: (b, i, 0)
    per_b = lambda b, i: (b, 0, 0)
    return pl.pallas_call(
        functools.partial(_out_proj_kernel, sub=sub),
        grid=(nb, length // tm),
        in_specs=[
            pl.BlockSpec((1, tm, k1), tok),
            pl.BlockSpec((1, tm, k2), tok),
            pl.BlockSpec((k1, d), lambda b, i: (0, 0)),
            pl.BlockSpec((k2, d), lambda b, i: (0, 0)),
            pl.BlockSpec((1, tm, d), tok),
            pl.BlockSpec((1, 1, d), per_b),
            pl.BlockSpec((1, d), lambda b, i: (0, 0)),
            pl.BlockSpec((1, 1, d), per_b),
            pl.BlockSpec((1, 1, d), per_b),
        ],
        out_specs=[pl.BlockSpec((1, tm, d), tok), pl.BlockSpec((1, tm, d), tok)],
        out_shape=[jax.ShapeDtypeStruct((nb, length, d), F32),
                   jax.ShapeDtypeStruct((nb, length, d), BF16)],
        compiler_params=_params("arbitrary", "arbitrary"),
        name="out_proj",
    )(y_ssd, y_sc, w1, w2, x, gate, g.reshape(1, d), shift, scale)


def _ffn_up_kernel(h_ref, wg_ref, wu_ref, o_ref, *, sub_m):
    tm = h_ref.shape[1]
    for m0 in range(0, tm, sub_m):
        h = h_ref[0, m0:m0 + sub_m, :]
        o_ref[0, m0:m0 + sub_m, :] = (_silu(_dot(h, wg_ref[...])) * _dot(h, wu_ref[...])).astype(o_ref.dtype)


def ffn_up(h, w_gate, w_up, tm=2048, tn=512, sub_m=512):
    nb, length, d = h.shape
    n = w_gate.shape[1]
    tm = min(tm, length)
    return pl.pallas_call(
        functools.partial(_ffn_up_kernel, sub_m=sub_m),
        grid=(nb, length // tm, n // tn),
        in_specs=[
            pl.BlockSpec((1, tm, d), lambda b, i, j: (b, i, 0)),
            pl.BlockSpec((d, tn), lambda b, i, j: (0, j)),
            pl.BlockSpec((d, tn), lambda b, i, j: (0, j)),
        ],
        out_specs=pl.BlockSpec((1, tm, tn), lambda b, i, j: (b, i, j)),
        out_shape=jax.ShapeDtypeStruct((nb, length, n), BF16),
        compiler_params=_params("arbitrary", "arbitrary", "arbitrary"),
        name="ffn_up",
    )(h, w_gate, w_up)


def _ffn_down_kernel(a_ref, w_ref, x_ref, gate_ref, g_ref, sh_ref, sc_ref, *refs, emit_x):
    if emit_x:
        xo_ref, h_ref, acc_ref = refs
    else:
        h_ref, acc_ref = refs
    k = pl.program_id(2)
    part = _dot(a_ref[0], w_ref[...])

    @pl.when(k == 0)
    def _():
        acc_ref[...] = part

    @pl.when(k > 0)
    def _():
        acc_ref[...] += part

    @pl.when(k == pl.num_programs(2) - 1)
    def _():
        xn = x_ref[0] + gate_ref[0] * acc_ref[...]
        if emit_x:
            xo_ref[0] = xn
        hn = _norm_rows(xn, g_ref[...])
        h_ref[0] = (hn * (1.0 + sc_ref[0]) + sh_ref[0]).astype(h_ref.dtype)


def ffn_down(act, w_down, x, gate, g, shift, scale, emit_x, h_dtype, tm=512, tk=1408):
    nb, length, d = x.shape
    kk = act.shape[-1]
    tok = lambda b, i, k: (b, i, 0)
    per_b = lambda b, i, k: (b, 0, 0)
    h_shape = jax.ShapeDtypeStruct((nb, length, d), h_dtype)
    out_shape = [jax.ShapeDtypeStruct((nb, length, d), F32), h_shape] if emit_x else [h_shape]
    return pl.pallas_call(
        functools.partial(_ffn_down_kernel, emit_x=emit_x),
        grid=(nb, length // tm, kk // tk),
        in_specs=[
            pl.BlockSpec((1, tm, tk), lambda b, i, k: (b, i, k)),
            pl.BlockSpec((tk, d), lambda b, i, k: (k, 0)),
            pl.BlockSpec((1, tm, d), tok),
            pl.BlockSpec((1, 1, d), per_b),
            pl.BlockSpec((1, d), lambda b, i, k: (0, 0)),
            pl.BlockSpec((1, 1, d), per_b),
            pl.BlockSpec((1, 1, d), per_b),
        ],
        out_specs=[pl.BlockSpec((1, tm, d), tok)] * len(out_shape),
        out_shape=out_shape,
        scratch_shapes=[pltpu.VMEM((tm, d), F32)],
        compiler_params=_params("arbitrary", "arbitrary", "arbitrary"),
        name="ffn_down",
    )(act, w_down, x, gate, g.reshape(1, d), shift, scale)


def _to_col_major(t, rows):
    b, length, ch = t.shape
    return t.reshape(b, rows, GRID_W, ch).transpose(0, 2, 1, 3).reshape(b, length, ch)


def _from_col_major(t, rows):
    b, length, ch = t.shape
    return t.reshape(b, GRID_W, rows, ch).transpose(0, 2, 1, 3).reshape(b, length, ch)


def _mixer(h_rows, nseq, w_main, w_dt, layer, h0_f, h0_b):
    (conv_w, conv_b, dt_bias, a_log, d_skip, ssd_g, sc_w) = layer
    nb, length, _ = h_rows.shape
    t = nb * length // nseq
    tm = min(length, 1024)
    n_main = w_main.shape[1]
    p = matmul(h_rows, w_main, BF16, tm=tm, tn=n_main // 2, sub=256)
    dt_raw = matmul(h_rows, w_dt, F32, tm=tm, tn=LANES, sub=LANES)
    p = p.reshape(nseq, t, n_main)
    dt_raw = dt_raw.reshape(nseq, t, LANES)
    xbc = conv_xbc(p, conv_w, conv_b)
    y_ssd, h_f, h_b = ssd(xbc, dt_raw, p, h0_f, h0_b, dt_bias, a_log, d_skip, ssd_g)
    y_sc = conv_sc(p, sc_w) if n_main == MAIN_COLS else None
    return y_ssd, y_sc, h_f, h_b


def kernel(x, c, ctx, c_ctx, ada_w, ada_b, mix_norm_g, w_in, ssd_conv_w, ssd_conv_b, ssd_dt_bias,
           ssd_a_log, ssd_d, ssd_norm_g, sc_conv_w, w_out, ffn_norm_g, w_gate, w_up, w_down,
           final_norm_g):
    batch, seq, d = x.shape
    depth = ada_w.shape[0]
    ctx_len = ctx.shape[1]
    rows = seq // GRID_W
    n_ctx_rows = batch * ctx_len

    cc = jnp.zeros((16, d), F32).at[:batch].set(c).at[batch].set(c_ctx)
    mods = adaln(cc, ada_w, ada_b)

    def mod_vectors(l):
        mx = [mods[l, :batch, i * d:(i + 1) * d].reshape(batch, 1, d) for i in range(6)]
        mc = [mods[l, batch:batch + 1, i * d:(i + 1) * d].reshape(1, 1, d) for i in range(6)]
        return mx, mc

    dt_lo = SSD_WIDTH + SSD_XBC
    w_main = [jnp.concatenate([w_in[l][:, :dt_lo], w_in[l][:, SSD_COLS:]], axis=1).astype(BF16)
              for l in range(depth)]
    w_dt = [jnp.zeros((d, LANES), BF16).at[:, :2 * SSD_HEADS].set(w_in[l][:, dt_lo:SSD_COLS].astype(BF16))
            for l in range(depth)]
    w_out_b = w_out.astype(BF16)
    w_gate_b = w_gate.astype(BF16)
    w_up_b = w_up.astype(BF16)
    w_down_b = w_down.astype(BF16)

    h_ctx = ctx.reshape(1, n_ctx_rows, d)
    mx, mc = mod_vectors(0)
    hx = norm_mod(x, mix_norm_g[0], mx[0], mx[1])
    hc = norm_mod(h_ctx, mix_norm_g[0], mc[0], mc[1])
    zero_state = jnp.zeros((batch, SSD_STATE, SSD_WIDTH), F32)
    zeros_d = jnp.zeros((batch, 1, d), F32)
    out = None

    for l in range(depth):
        last = l == depth - 1
        layer = (ssd_conv_w[l], ssd_conv_b[l], ssd_dt_bias[l], ssd_a_log[l], ssd_d[l],
                 ssd_norm_g[l], sc_conv_w[l])
        w_o1, w_o2 = w_out_b[l, :SSD_WIDTH], w_out_b[l, SSD_WIDTH:]
        if not last:
            mx_next, mc_next = mod_vectors(l + 1)

        if last:
            _, _, state_f, state_b = _mixer(hc, batch, w_main[l][:, :dt_lo], w_dt[l], layer,
                                            zero_state, zero_state)
        else:
            y_ssd, y_sc, state_f, state_b = _mixer(hc, batch, w_main[l], w_dt[l], layer,
                                                   zero_state, zero_state)
            h_ctx, hf = out_proj(y_ssd.reshape(1, n_ctx_rows, -1), y_sc.reshape(1, n_ctx_rows, -1),
                                 w_o1, w_o2, h_ctx, mc[2], ffn_norm_g[l], mc[3], mc[4])
            act = ffn_up(hf, w_gate_b[l], w_up_b[l])
            h_ctx, hc = ffn_down(act, w_down_b[l], h_ctx, mc[5], mix_norm_g[l + 1],
                                 mc_next[0], mc_next[1], emit_x=True, h_dtype=BF16)

        col_major = l % 2 == 1
        if col_major:
            hx = _to_col_major(hx, rows)
        y_ssd, y_sc, _, _ = _mixer(hx, batch, w_main[l], w_dt[l], layer, state_f, state_b)
        if col_major:
            y_ssd = _from_col_major(y_ssd, rows)
            y_sc = _from_col_major(y_sc, rows)
        x, hf = out_proj(y_ssd, y_sc, w_o1, w_o2, x, mx[2], ffn_norm_g[l], mx[3], mx[4])
        act = ffn_up(hf, w_gate_b[l], w_up_b[l])
        if last:
            (out,) = ffn_down(act, w_down_b[l], x, mx[5], final_norm_g, zeros_d, zeros_d,
                              emit_x=False, h_dtype=x.dtype)
        else:
            x, hx = ffn_down(act, w_down_b[l], x, mx[5], mix_norm_g[l + 1],
                             mx_next[0], mx_next[1], emit_x=True, h_dtype=BF16)
            mx, mc = mx_next, mc_next
    return out
```

```python
import functools

import jax
import jax.numpy as jnp
from jax import lax
from jax.experimental import pallas as pl
from jax.experimental.pallas import tpu as pltpu

F32 = jnp.float32
BF16 = jnp.bfloat16

D_MODEL = 2048
GRID_W = 64
NORM_EPS = 1e-6
SSD_WIDTH = 1024
SSD_HEAD_DIM = 64
SSD_HEADS = 16
SSD_GROUPS = 2
SSD_STATE = 128
SSD_CONV_W = 5
SSD_CHUNK = 128
SSD_GN = SSD_GROUPS * SSD_STATE
SSD_XBC = SSD_WIDTH + 2 * SSD_GN
SSD_COLS = SSD_WIDTH + SSD_XBC + 2 * SSD_HEADS
SC_WIDTH = 1024
SC_CONV_W = 3
MAIN_COLS = SSD_WIDTH + SSD_XBC + 3 * SC_WIDTH
FFN_HIDDEN = 5632
LANES = 128
HEAD_GROUP_COLS = SSD_WIDTH // SSD_GROUPS
VMEM_LIMIT = 56 * 1024 * 1024


def _params(*sem):
    return pltpu.CompilerParams(dimension_semantics=sem, vmem_limit_bytes=VMEM_LIMIT)


def _silu(v):
    return v * jax.nn.sigmoid(v)


def _softplus(v):
    return jnp.maximum(v, 0.0) + jnp.log1p(jnp.exp(-jnp.abs(v)))


def _norm_rows(v, g):
    ms = jnp.mean(v * v, axis=-1, keepdims=True)
    return v * lax.rsqrt(ms + NORM_EPS) * g


def _dot(a, b):
    return jnp.dot(a, b, preferred_element_type=F32)


def _adaln_kernel(c_ref, w_ref, b_ref, o_ref):
    s = _silu(c_ref[...]).astype(BF16)
    o_ref[0] = _dot(s, w_ref[0].astype(BF16)) + b_ref[0]


def adaln(cc, ada_w, ada_b, tn=1536):
    depth, d, n = ada_w.shape
    rows = cc.shape[0]
    return pl.pallas_call(
        _adaln_kernel,
        grid=(depth, n // tn),
        in_specs=[
            pl.BlockSpec((rows, d), lambda l, j: (0, 0)),
            pl.BlockSpec((1, d, tn), lambda l, j: (l, 0, j)),
            pl.BlockSpec((1, 1, tn), lambda l, j: (l, 0, j)),
        ],
        out_specs=pl.BlockSpec((1, rows, tn), lambda l, j: (l, 0, j)),
        out_shape=jax.ShapeDtypeStruct((depth, rows, n), F32),
        compiler_params=_params("arbitrary", "arbitrary"),
        name="adaln",
    )(cc, ada_w, ada_b.reshape(depth, 1, n))


def _norm_mod_kernel(x_ref, g_ref, sh_ref, sc_ref, o_ref):
    y = _norm_rows(x_ref[0], g_ref[...])
    o_ref[0] = (y * (1.0 + sc_ref[0]) + sh_ref[0]).astype(o_ref.dtype)


def norm_mod(x, g, shift, scale, tr=512):
    nb, length, d = x.shape
    return pl.pallas_call(
        _norm_mod_kernel,
        grid=(nb, length // tr),
        in_specs=[
            pl.BlockSpec((1, tr, d), lambda b, i: (b, i, 0)),
            pl.BlockSpec((1, d), lambda b, i: (0, 0)),
            pl.BlockSpec((1, 1, d), lambda b, i: (b, 0, 0)),
            pl.BlockSpec((1, 1, d), lambda b, i: (b, 0, 0)),
        ],
        out_specs=pl.BlockSpec((1, tr, d), lambda b, i: (b, i, 0)),
        out_shape=jax.ShapeDtypeStruct((nb, length, d), BF16),
        compiler_params=_params("arbitrary", "arbitrary"),
        name="norm_mod",
    )(x, g.reshape(1, d), shift, scale)


def _matmul_kernel(a_ref, w_ref, o_ref, *, sub):
    a = a_ref[0]
    tn = o_ref.shape[-1]
    for n0 in range(0, tn, sub):
        o_ref[0, :, n0:n0 + sub] = _dot(a, w_ref[:, n0:n0 + sub]).astype(o_ref.dtype)


def matmul(a, w, out_dtype, tm, tn, sub):
    nb, length, k = a.shape
    n = w.shape[1]
    return pl.pallas_call(
        functools.partial(_matmul_kernel, sub=sub),
        grid=(n // tn, nb, length // tm),
        in_specs=[
            pl.BlockSpec((1, tm, k), lambda j, b, i: (b, i, 0)),
            pl.BlockSpec((k, tn), lambda j, b, i: (0, j)),
        ],
        out_specs=pl.BlockSpec((1, tm, tn), lambda j, b, i: (b, i, j)),
        out_shape=jax.ShapeDtypeStruct((nb, length, n), out_dtype),
        compiler_params=_params("arbitrary", "arbitrary", "arbitrary"),
        name="matmul",
    )(a, w)


def _shifted_taps(pad_ref, r0, rows, taps):
    half = taps // 2
    win = pad_ref[r0:r0 + rows + 32, :]
    out = []
    for k in range(taps):
        shift = (half - k) % (rows + 32)
        rolled = win if shift == 0 else pltpu.roll(win, shift, axis=0)
        out.append(rolled[16:16 + rows, :])
    return out


def _fill_padded(pad_ref, vals):
    t = vals.shape[0]
    zeros = jnp.zeros((16, vals.shape[1]), F32)
    pad_ref[0:16, :] = zeros
    pad_ref[t + 16:t + 32, :] = zeros
    pad_ref[16:t + 16, :] = vals


def _conv_xbc_kernel(x_ref, w_ref, b_ref, o_ref, pad_ref, *, rows):
    t = x_ref.shape[1]
    _fill_padded(pad_ref, x_ref[0].astype(F32))
    for r0 in range(0, t, rows):
        taps = _shifted_taps(pad_ref, r0, rows, SSD_CONV_W)
        acc = b_ref[...] + w_ref[0:1, :] * taps[0]
        for k in range(1, SSD_CONV_W):
            acc = acc + w_ref[k:k + 1, :] * taps[k]
        o_ref[0, r0:r0 + rows, :] = _silu(acc).astype(o_ref.dtype)


def conv_xbc(p, conv_w, conv_b, ct=256):
    nseq, t, _ = p.shape
    off = SSD_WIDTH // ct
    w8 = jnp.zeros((8, SSD_XBC), F32).at[:SSD_CONV_W].set(conv_w)
    return pl.pallas_call(
        functools.partial(_conv_xbc_kernel, rows=min(t, 256)),
        grid=(nseq, SSD_XBC // ct),
        in_specs=[
            pl.BlockSpec((1, t, ct), lambda s, j: (s, 0, off + j)),
            pl.BlockSpec((8, ct), lambda s, j: (0, j)),
            pl.BlockSpec((1, ct), lambda s, j: (0, j)),
        ],
        out_specs=pl.BlockSpec((1, t, ct), lambda s, j: (s, 0, j)),
        out_shape=jax.ShapeDtypeStruct((nseq, t, SSD_XBC), BF16),
        scratch_shapes=[pltpu.VMEM((t + 32, ct), F32)],
        compiler_params=_params("arbitrary", "arbitrary"),
        name="conv_xbc",
    )(p, w8, conv_b.reshape(1, SSD_XBC))


def _conv_sc_kernel(gb_ref, gc_ref, v_ref, w_ref, o_ref, pad_ref, *, rows):
    t = v_ref.shape[1]
    _fill_padded(pad_ref, gc_ref[0].astype(F32) * v_ref[0].astype(F32))
    for r0 in range(0, t, rows):
        taps = _shifted_taps(pad_ref, r0, rows, SC_CONV_W)
        acc = w_ref[0:1, :] * taps[0]
        for k in range(1, SC_CONV_W):
            acc = acc + w_ref[k:k + 1, :] * taps[k]
        gate = gb_ref[0, r0:r0 + rows, :].astype(F32)
        o_ref[0, r0:r0 + rows, :] = (gate * acc).astype(o_ref.dtype)


def conv_sc(p, sc_w, ct=256):
    nseq, t, _ = p.shape
    base = (SSD_WIDTH + SSD_XBC) // ct
    step = SC_WIDTH // ct
    w8 = jnp.zeros((8, SC_WIDTH), F32).at[:SC_CONV_W].set(sc_w)
    col = lambda k: pl.BlockSpec((1, t, ct), lambda s, j: (s, 0, base + k * step + j))
    return pl.pallas_call(
        functools.partial(_conv_sc_kernel, rows=min(t, 256)),
        grid=(nseq, SC_WIDTH // ct),
        in_specs=[col(0), col(1), col(2), pl.BlockSpec((8, ct), lambda s, j: (0, j))],
        out_specs=pl.BlockSpec((1, t, ct), lambda s, j: (s, 0, j)),
        out_shape=jax.ShapeDtypeStruct((nseq, t, SC_WIDTH), BF16),
        scratch_shapes=[pltpu.VMEM((t + 32, ct), F32)],
        compiler_params=_params("arbitrary", "arbitrary"),
        name="conv_sc",
    )(p, p, p, w8)


def _cumsum_rows(v, reverse):
    n = v.shape[0]
    row = lax.broadcasted_iota(jnp.int32, v.shape, 0)
    d = 1
    while d < n:
        if reverse:
            v = v + jnp.where(row < n - d, pltpu.roll(v, n - d, axis=0), 0.0)
        else:
            v = v + jnp.where(row >= d, pltpu.roll(v, d, axis=0), 0.0)
        d *= 2
    return v


def _expand_heads(q, off):
    r = q.shape[0]
    lane = lax.broadcasted_iota(jnp.int32, (r, LANES), 1)
    pieces = []
    for i in range(SSD_HEADS // 2):
        lo = jnp.broadcast_to(q[:, off + 2 * i:off + 2 * i + 1], (r, LANES))
        hi = jnp.broadcast_to(q[:, off + 2 * i + 1:off + 2 * i + 2], (r, LANES))
        pieces.append(jnp.where(lane < SSD_HEAD_DIM, lo, hi))
    return jnp.concatenate(pieces, axis=1)


def _state_update(st_ref, b_mat, xw, decay_row):
    for g in range(SSD_GROUPS):
        cols = slice(g * HEAD_GROUP_COLS, (g + 1) * HEAD_GROUP_COLS)
        contrib = lax.dot_general(
            b_mat[:, g * SSD_STATE:(g + 1) * SSD_STATE], xw[:, cols],
            (((0,), (0,)), ((), ())), preferred_element_type=F32)
        st_ref[:, cols] = st_ref[:, cols] * decay_row[:, cols] + contrib


def _ssd_kernel(xbc_ref, dt_ref, z_ref, h0f_ref, h0b_ref, dtb_ref, alog_ref, dsk_ref, g_ref,
                y_ref, hfl_ref, hbl_ref, hbe_ref, stf_ref, stb_ref, *, nc):
    q = SSD_CHUNK
    j = pl.program_id(1)
    fwd_lanes = 0
    bwd_lanes = SSD_HEADS

    dt = _softplus(dt_ref[0] + dtb_ref[...])
    d_a = dt * (-jnp.exp(alog_ref[...]))
    xs_bf = xbc_ref[0, :, 0:SSD_WIDTH]
    xs = xs_bf.astype(F32)
    b_mat = xbc_ref[0, :, SSD_WIDTH:SSD_WIDTH + SSD_GN]

    @pl.when(j == 0)
    def _():
        stf_ref[...] = h0f_ref[0]
        stb_ref[...] = h0b_ref[0]

    @pl.when(j < nc)
    def _backward_states():
        c = nc - 1 - j
        acs_b = _cumsum_rows(d_a, reverse=True)
        tot = acs_b[0:1, :]
        w_full = _expand_heads(dt * jnp.exp(tot - acs_b), bwd_lanes)
        xw = (xs * w_full).astype(BF16)
        decay = _expand_heads(jnp.exp(jnp.broadcast_to(tot, (8, LANES))), bwd_lanes)[0:1, :]
        hbe_ref[c] = stb_ref[...].astype(BF16)
        _state_update(stb_ref, b_mat, xw, decay)

        @pl.when(j == nc - 1)
        def _():
            hbl_ref[0] = stb_ref[...]

    @pl.when(j >= nc)
    def _forward_outputs():
        c = j - nc
        row = lax.broadcasted_iota(jnp.int32, (q, q), 0)
        lane = lax.broadcasted_iota(jnp.int32, (q, q), 1)
        acs_f = _cumsum_rows(d_a, reverse=False)
        acs_b = _cumsum_rows(d_a, reverse=True)
        acs = jnp.where(lane < bwd_lanes, acs_f, acs_b)
        acs_t = acs.T
        dt_t = dt.T
        c_mat = xbc_ref[0, :, SSD_WIDTH + SSD_GN:SSD_XBC]
        causal = row >= lane
        anti = row <= lane

        cb = []
        for g in range(SSD_GROUPS):
            gs = slice(g * SSD_STATE, (g + 1) * SSD_STATE)
            cb.append(lax.dot_general(c_mat[:, gs], b_mat[:, gs], (((1,), (1,)), ((), ())),
                                      preferred_element_type=F32))

        y_parts = []
        for i in range(SSD_HEADS // 2):
            mats = []
            for h in (2 * i, 2 * i + 1):
                hf, hb = fwd_lanes + h, bwd_lanes + h
                seg_f = acs[:, hf:hf + 1] - acs_t[hf:hf + 1, :]
                seg_b = acs[:, hb:hb + 1] - acs_t[hb:hb + 1, :]
                l_f = jnp.exp(jnp.where(causal, seg_f, -jnp.inf)) * dt_t[hf:hf + 1, :]
                l_b = jnp.exp(jnp.where(anti, seg_b, -jnp.inf)) * dt_t[hb:hb + 1, :]
                mats.append((cb[h // (SSD_HEADS // SSD_GROUPS)] * (l_f + l_b)).astype(BF16))
            pair = xs[:, i * LANES:(i + 1) * LANES]
            top = jnp.where(lane < SSD_HEAD_DIM, pair, 0.0).astype(BF16)
            bot = jnp.where(lane >= SSD_HEAD_DIM, pair, 0.0).astype(BF16)
            y_parts.append(_dot(jnp.concatenate(mats, axis=1), jnp.concatenate([top, bot], axis=0)))
        y = jnp.concatenate(y_parts, axis=1)

        e_acs = jnp.exp(acs)
        e_f = _expand_heads(e_acs, fwd_lanes)
        e_b = _expand_heads(e_acs, bwd_lanes)
        st_f = stf_ref[...].astype(BF16)
        st_b = hbe_ref[c]
        off_f, off_b = [], []
        for g in range(SSD_GROUPS):
            gs = slice(g * SSD_STATE, (g + 1) * SSD_STATE)
            cols = slice(g * HEAD_GROUP_COLS, (g + 1) * HEAD_GROUP_COLS)
            off_f.append(_dot(c_mat[:, gs], st_f[:, cols]))
            off_b.append(_dot(c_mat[:, gs], st_b[:, cols]))
        y = y + e_f * jnp.concatenate(off_f, axis=1) + e_b * jnp.concatenate(off_b, axis=1)
        y = y + dsk_ref[...] * xs

        tot = acs_f[q - 1:q, :]
        w_full = _expand_heads(dt * jnp.exp(tot - acs_f), fwd_lanes)
        _state_update(stf_ref, b_mat, (xs * w_full).astype(BF16), e_f[q - 1:q, :])

        z = z_ref[0].astype(F32)
        y_ref[0] = _norm_rows(y * _silu(z), g_ref[...]).astype(y_ref.dtype)

        @pl.when(j == 2 * nc - 1)
        def _():
            hfl_ref[0] = stf_ref[...]


def ssd(xbc, dt_raw, p, h0_f, h0_b, dt_bias, a_log, d_skip, norm_g):
    nseq, t, _ = xbc.shape
    nc = t // SSD_CHUNK
    pad_row = lambda v: jnp.zeros((1, LANES), F32).at[0, :2 * SSD_HEADS].set(v.reshape(-1))
    chunk = lambda s, j: (s, jnp.where(j < nc, nc - 1 - j, j - nc), 0)
    out_chunk = lambda s, j: (s, jnp.maximum(j - nc, 0), 0)
    const2 = lambda s, j: (0, 0)
    seq = lambda s, j: (s, 0, 0)
    state_shape = jax.ShapeDtypeStruct((nseq, SSD_STATE, SSD_WIDTH), F32)
    return pl.pallas_call(
        functools.partial(_ssd_kernel, nc=nc),
        grid=(nseq, 2 * nc),
        in_specs=[
            pl.BlockSpec((1, SSD_CHUNK, SSD_XBC), chunk),
            pl.BlockSpec((1, SSD_CHUNK, LANES), chunk),
            pl.BlockSpec((1, SSD_CHUNK, SSD_WIDTH), out_chunk),
            pl.BlockSpec((1, SSD_STATE, SSD_WIDTH), seq),
            pl.BlockSpec((1, SSD_STATE, SSD_WIDTH), seq),
            pl.BlockSpec((1, LANES), const2),
            pl.BlockSpec((1, LANES), const2),
            pl.BlockSpec((1, SSD_WIDTH), const2),
            pl.BlockSpec((1, SSD_WIDTH), const2),
        ],
        out_specs=[
            pl.BlockSpec((1, SSD_CHUNK, SSD_WIDTH), out_chunk),
            pl.BlockSpec((1, SSD_STATE, SSD_WIDTH), seq),
            pl.BlockSpec((1, SSD_STATE, SSD_WIDTH), seq),
        ],
        out_shape=[jax.ShapeDtypeStruct((nseq, t, SSD_WIDTH), BF16), state_shape, state_shape],
        scratch_shapes=[
            pltpu.VMEM((nc, SSD_STATE, SSD_WIDTH), BF16),
            pltpu.VMEM((SSD_STATE, SSD_WIDTH), F32),
            pltpu.VMEM((SSD_STATE, SSD_WIDTH), F32),
        ],
        compiler_params=_params("arbitrary", "arbitrary"),
        name="ssd",
    )(xbc, dt_raw, p, h0_f, h0_b, pad_row(dt_bias), pad_row(a_log),
      jnp.repeat(d_skip, SSD_HEAD_DIM).reshape(1, SSD_WIDTH), norm_g.reshape(1, SSD_WIDTH))


def _out_proj_kernel(y1_ref, y2_ref, w1_ref, w2_ref, x_ref, gate_ref, g_ref, sh_ref, sc_ref,
                     xo_ref, h_ref, *, sub):
    y1 = y1_ref[0]
    y2 = y2_ref[0]
    n = xo_ref.shape[-1]
    for n0 in range(0, n, sub):
        ns = slice(n0, n0 + sub)
        acc = _dot(y1, w1_ref[:, ns]) + _dot(y2, w2_ref[:, ns])
        xo_ref[0, :, ns] = x_ref[0, :, ns] + gate_ref[0, :, ns] * acc
    hn = _norm_rows(xo_ref[0], g_ref[...])
    h_ref[0] = (hn * (1.0 + sc_ref[0]) + sh_ref[0]).astype(h_ref.dtype)


def out_proj(y_ssd, y_sc, w1, w2, x, gate, g, shift, scale, tm=512, sub=1024):
    nb, length, d = x.shape
    k1, k2 = y_ssd.shape[-1], y_sc.shape[-1]
    tok = lambda b, i: (b, i, 0)
    per_b = lambda b, i: (b, 0, 0)
    return pl.pallas_call(
        functools.partial(_out_proj_kernel, sub=sub),
        grid=(nb, length // tm),
        in_specs=[
            pl.BlockSpec((1, tm, k1), tok),
            pl.BlockSpec((1, tm, k2), tok),
            pl.BlockSpec((k1, d), lambda b, i: (0, 0)),
            pl.BlockSpec((k2, d), lambda b, i: (0, 0)),
            pl.BlockSpec((1, tm, d), tok),
            pl.BlockSpec((1, 1, d), per_b),
            pl.BlockSpec((1, d), lambda b, i: (0, 0)),
            pl.BlockSpec((1, 1, d), per_b),
            pl.BlockSpec((1, 1, d), per_b),
        ],
        out_specs=[pl.BlockSpec((1, tm, d), tok), pl.BlockSpec((1, tm, d), tok)],
        out_shape=[jax.ShapeDtypeStruct((nb, length, d), F32),
                   jax.ShapeDtypeStruct((nb, length, d), BF16)],
        compiler_params=_params("arbitrary", "arbitrary"),
        name="out_proj",
    )(y_ssd, y_sc, w1, w2, x, gate, g.reshape(1, d), shift, scale)


def _ffn_up_kernel(h_ref, wg_ref, wu_ref, o_ref, wgb_ref, wub_ref, *, sub_m):
    @pl.when((pl.program_id(1) == 0) & (pl.program_id(2) == 0))
    def _():
        wgb_ref[...] = wg_ref[0].astype(BF16)
        wub_ref[...] = wu_ref[0].astype(BF16)

    tm = h_ref.shape[1]
    for m0 in range(0, tm, sub_m):
        h = h_ref[0, m0:m0 + sub_m, :]
        o_ref[0, m0:m0 + sub_m, :] = (_silu(_dot(h, wgb_ref[...])) * _dot(h, wub_ref[...])).astype(o_ref.dtype)


def ffn_up(h, w_gate, w_up, layer, tm=2048, tn=512, sub_m=512):
    nb, length, d = h.shape
    n = w_gate.shape[-1]
    tm = min(tm, length)
    w_spec = pl.BlockSpec((1, d, tn), lambda j, b, i: (layer, 0, j))
    return pl.pallas_call(
        functools.partial(_ffn_up_kernel, sub_m=sub_m),
        grid=(n // tn, nb, length // tm),
        in_specs=[pl.BlockSpec((1, tm, d), lambda j, b, i: (b, i, 0)), w_spec, w_spec],
        out_specs=pl.BlockSpec((1, tm, tn), lambda j, b, i: (b, i, j)),
        out_shape=jax.ShapeDtypeStruct((nb, length, n), BF16),
        scratch_shapes=[pltpu.VMEM((d, tn), BF16), pltpu.VMEM((d, tn), BF16)],
        compiler_params=_params("arbitrary", "arbitrary", "arbitrary"),
        name="ffn_up",
    )(h, w_gate, w_up)


def _ffn_down_kernel(a_ref, w_ref, x_ref, gate_ref, g_ref, sh_ref, sc_ref, *refs, emit_x, sub_n, sub_m):
    if emit_x:
        acc_ref, h_ref = refs
    else:
        (acc_ref,) = refs
        h_ref = acc_ref
    k = pl.program_id(2)
    tm, n = acc_ref.shape[1], acc_ref.shape[2]
    a = a_ref[0]
    for n0 in range(0, n, sub_n):
        ns = slice(n0, n0 + sub_n)
        part = _dot(a, w_ref[:, ns])

        @pl.when(k == 0)
        def _():
            acc_ref[0, :, ns] = part

        @pl.when(k > 0)
        def _():
            acc_ref[0, :, ns] += part

    @pl.when(k == pl.num_programs(2) - 1)
    def _():
        for m0 in range(0, tm, sub_m):
            ms = slice(m0, m0 + sub_m)
            xn = x_ref[0, ms, :] + gate_ref[0] * acc_ref[0, ms, :]
            if emit_x:
                acc_ref[0, ms, :] = xn
            hn = _norm_rows(xn, g_ref[...])
            h_ref[0, ms, :] = (hn * (1.0 + sc_ref[0]) + sh_ref[0]).astype(h_ref.dtype)


def ffn_down(act, w_down, x, gate, g, shift, scale, emit_x, h_dtype, tm=1024, tk=512):
    nb, length, d = x.shape
    kk = act.shape[-1]
    tok = lambda b, i, k: (b, i, 0)
    per_b = lambda b, i, k: (b, 0, 0)
    if emit_x:
        out_shape = [jax.ShapeDtypeStruct((nb, length, d), F32), jax.ShapeDtypeStruct((nb, length, d), h_dtype)]
    else:
        assert h_dtype == F32
        out_shape = [jax.ShapeDtypeStruct((nb, length, d), F32)]
    return pl.pallas_call(
        functools.partial(_ffn_down_kernel, emit_x=emit_x, sub_n=d // 2, sub_m=256),
        grid=(nb, length // tm, kk // tk),
        in_specs=[
            pl.BlockSpec((1, tm, tk), lambda b, i, k: (b, i, k)),
            pl.BlockSpec((tk, d), lambda b, i, k: (k, 0)),
            pl.BlockSpec((1, tm, d), tok),
            pl.BlockSpec((1, 1, d), per_b),
            pl.BlockSpec((1, d), lambda b, i, k: (0, 0)),
            pl.BlockSpec((1, 1, d), per_b),
            pl.BlockSpec((1, 1, d), per_b),
        ],
        out_specs=[pl.BlockSpec((1, tm, d), tok)] * len(out_shape),
        out_shape=out_shape,
        compiler_params=_params("arbitrary", "arbitrary", "arbitrary"),
        name="ffn_down",
    )(act, w_down, x, gate, g.reshape(1, d), shift, scale)


def _to_col_major(t, rows):
    b, length, ch = t.shape
    return t.reshape(b, rows, GRID_W, ch).transpose(0, 2, 1, 3).reshape(b, length, ch)


def _from_col_major(t, rows):
    b, length, ch = t.shape
    return t.reshape(b, GRID_W, rows, ch).transpose(0, 2, 1, 3).reshape(b, length, ch)


def _mixer(h_rows, nseq, w_main, w_dt, layer, h0_f, h0_b):
    (conv_w, conv_b, dt_bias, a_log, d_skip, ssd_g, sc_w) = layer
    nb, length, _ = h_rows.shape
    t = nb * length // nseq
    tm = min(length, 1024)
    n_main = w_main.shape[1]
    p = matmul(h_rows, w_main, BF16, tm=tm, tn=n_main // 2, sub=256)
    dt_raw = matmul(h_rows, w_dt, F32, tm=tm, tn=LANES, sub=LANES)
    p = p.reshape(nseq, t, n_main)
    dt_raw = dt_raw.reshape(nseq, t, LANES)
    xbc = conv_xbc(p, conv_w, conv_b)
    y_ssd, h_f, h_b = ssd(xbc, dt_raw, p, h0_f, h0_b, dt_bias, a_log, d_skip, ssd_g)
    y_sc = conv_sc(p, sc_w) if n_main == MAIN_COLS else None
    return y_ssd, y_sc, h_f, h_b


def kernel(x, c, ctx, c_ctx, ada_w, ada_b, mix_norm_g, w_in, ssd_conv_w, ssd_conv_b, ssd_dt_bias,
           ssd_a_log, ssd_d, ssd_norm_g, sc_conv_w, w_out, ffn_norm_g, w_gate, w_up, w_down,
           final_norm_g):
    batch, seq, d = x.shape
    depth = ada_w.shape[0]
    ctx_len = ctx.shape[1]
    rows = seq // GRID_W
    n_ctx_rows = batch * ctx_len

    cc = jnp.zeros((16, d), F32).at[:batch].set(c).at[batch].set(c_ctx)
    mods = adaln(cc, ada_w, ada_b)

    def mod_vectors(l):
        mx = [mods[l, :batch, i * d:(i + 1) * d].reshape(batch, 1, d) for i in range(6)]
        mc = [mods[l, batch:batch + 1, i * d:(i + 1) * d].reshape(1, 1, d) for i in range(6)]
        return mx, mc

    dt_lo = SSD_WIDTH + SSD_XBC
    w_main = [jnp.concatenate([w_in[l][:, :dt_lo], w_in[l][:, SSD_COLS:]], axis=1).astype(BF16)
              for l in range(depth)]
    w_dt = [jnp.zeros((d, LANES), BF16).at[:, :2 * SSD_HEADS].set(w_in[l][:, dt_lo:SSD_COLS].astype(BF16))
            for l in range(depth)]
    w_out_b = w_out.astype(BF16)
    w_down_b = w_down.astype(BF16)

    h_ctx = ctx.reshape(1, n_ctx_rows, d)
    mx, mc = mod_vectors(0)
    hx = norm_mod(x, mix_norm_g[0], mx[0], mx[1])
    hc = norm_mod(h_ctx, mix_norm_g[0], mc[0], mc[1])
    zero_state = jnp.zeros((batch, SSD_STATE, SSD_WIDTH), F32)
    zeros_d = jnp.zeros((batch, 1, d), F32)
    out = None

    for l in range(depth):
        last = l == depth - 1
        layer = (ssd_conv_w[l], ssd_conv_b[l], ssd_dt_bias[l], ssd_a_log[l], ssd_d[l],
                 ssd_norm_g[l], sc_conv_w[l])
        w_o1, w_o2 = w_out_b[l, :SSD_WIDTH], w_out_b[l, SSD_WIDTH:]
        if not last:
            mx_next, mc_next = mod_vectors(l + 1)

        if last:
            _, _, state_f, state_b = _mixer(hc, batch, w_main[l][:, :dt_lo], w_dt[l], layer,
                                            zero_state, zero_state)
        else:
            y_ssd, y_sc, state_f, state_b = _mixer(hc, batch, w_main[l], w_dt[l], layer,
                                                   zero_state, zero_state)
            h_ctx, hf = out_proj(y_ssd.reshape(1, n_ctx_rows, -1), y_sc.reshape(1, n_ctx_rows, -1),
                                 w_o1, w_o2, h_ctx, mc[2], ffn_norm_g[l], mc[3], mc[4])
            act = ffn_up(hf, w_gate, w_up, l)
            h_ctx, hc = ffn_down(act, w_down_b[l], h_ctx, mc[5], mix_norm_g[l + 1],
                                 mc_next[0], mc_next[1], emit_x=True, h_dtype=BF16)

        col_major = l % 2 == 1
        if col_major:
            hx = _to_col_major(hx, rows)
        y_ssd, y_sc, _, _ = _mixer(hx, batch, w_main[l], w_dt[l], layer, state_f, state_b)
        if col_major:
            y_ssd = _from_col_major(y_ssd, rows)
            y_sc = _from_col_major(y_sc, rows)
        x, hf = out_proj(y_ssd, y_sc, w_o1, w_o2, x, mx[2], ffn_norm_g[l], mx[3], mx[4])
        act = ffn_up(hf, w_gate, w_up, l)
        if last:
            (out,) = ffn_down(act, w_down_b[l], x, mx[5], final_norm_g, zeros_d, zeros_d,
                              emit_x=False, h_dtype=x.dtype)
        else:
            x, hx = ffn_down(act, w_down_b[l], x, mx[5], mix_norm_g[l + 1],
                             mx_next[0], mx_next[1], emit_x=True, h_dtype=BF16)
            mx, mc = mx_next, mc_next
    return out
```

```python
import functools

import jax
import jax.numpy as jnp
import numpy as np
from jax import lax
from jax.experimental import pallas as pl
from jax.experimental.pallas import tpu as pltpu

F32 = jnp.float32
BF16 = jnp.bfloat16

D_MODEL = 2048
GRID_W = 64
NORM_EPS = 1e-6
SSD_WIDTH = 1024
SSD_HEAD_DIM = 64
SSD_HEADS = 16
SSD_GROUPS = 2
SSD_STATE = 128
SSD_CONV_W = 5
SSD_CHUNK = 128
SSD_GN = SSD_GROUPS * SSD_STATE
SSD_XBC = SSD_WIDTH + 2 * SSD_GN
SSD_COLS = SSD_WIDTH + SSD_XBC + 2 * SSD_HEADS
SC_WIDTH = 1024
SC_CONV_W = 3
MAIN_COLS = SSD_WIDTH + SSD_XBC + 3 * SC_WIDTH
FFN_HIDDEN = 5632
LANES = 128
HEAD_GROUP_COLS = SSD_WIDTH // SSD_GROUPS
VMEM_LIMIT = 56 * 1024 * 1024
LOG2_E = 1.4426950408889634


def _params(*sem):
    return pltpu.CompilerParams(dimension_semantics=sem, vmem_limit_bytes=VMEM_LIMIT)


def _silu(v):
    return v * jax.nn.sigmoid(v)


def _softplus(v):
    return jnp.maximum(v, 0.0) + jnp.log1p(jnp.exp(-jnp.abs(v)))


def _norm_rows(v, g):
    ms = jnp.mean(v * v, axis=-1, keepdims=True)
    return v * lax.rsqrt(ms + NORM_EPS) * g


def _dot(a, b):
    return jnp.dot(a, b, preferred_element_type=F32)


def _adaln_kernel(c_ref, w_ref, b_ref, o_ref):
    s = _silu(c_ref[...]).astype(BF16)
    o_ref[0] = _dot(s, w_ref[0].astype(BF16)) + b_ref[0]


def adaln(cc, ada_w, ada_b, tn=1536):
    depth, d, n = ada_w.shape
    rows = cc.shape[0]
    return pl.pallas_call(
        _adaln_kernel,
        grid=(depth, n // tn),
        in_specs=[
            pl.BlockSpec((rows, d), lambda l, j: (0, 0)),
            pl.BlockSpec((1, d, tn), lambda l, j: (l, 0, j)),
            pl.BlockSpec((1, 1, tn), lambda l, j: (l, 0, j)),
        ],
        out_specs=pl.BlockSpec((1, rows, tn), lambda l, j: (l, 0, j)),
        out_shape=jax.ShapeDtypeStruct((depth, rows, n), F32),
        compiler_params=_params("arbitrary", "arbitrary"),
        name="adaln",
    )(cc, ada_w, ada_b.reshape(depth, 1, n))


def _norm_mod_kernel(x_ref, g_ref, sh_ref, sc_ref, o_ref):
    y = _norm_rows(x_ref[0], g_ref[...])
    o_ref[0] = (y * (1.0 + sc_ref[0]) + sh_ref[0]).astype(o_ref.dtype)


def norm_mod(x, g, shift, scale, tr=512):
    nb, length, d = x.shape
    return pl.pallas_call(
        _norm_mod_kernel,
        grid=(nb, length // tr),
        in_specs=[
            pl.BlockSpec((1, tr, d), lambda b, i: (b, i, 0)),
            pl.BlockSpec((1, d), lambda b, i: (0, 0)),
            pl.BlockSpec((1, 1, d), lambda b, i: (b, 0, 0)),
            pl.BlockSpec((1, 1, d), lambda b, i: (b, 0, 0)),
        ],
        out_specs=pl.BlockSpec((1, tr, d), lambda b, i: (b, i, 0)),
        out_shape=jax.ShapeDtypeStruct((nb, length, d), BF16),
        compiler_params=_params("arbitrary", "arbitrary"),
        name="norm_mod",
    )(x, g.reshape(1, d), shift, scale)


def _matmul_kernel(a_ref, w_ref, o_ref, *, sub):
    a = a_ref[0]
    tn = o_ref.shape[-1]
    for n0 in range(0, tn, sub):
        o_ref[0, :, n0:n0 + sub] = _dot(a, w_ref[:, n0:n0 + sub]).astype(o_ref.dtype)


def matmul(a, w, out_dtype, tm, tn, sub):
    nb, length, k = a.shape
    n = w.shape[1]
    return pl.pallas_call(
        functools.partial(_matmul_kernel, sub=sub),
        grid=(n // tn, nb, length // tm),
        in_specs=[
            pl.BlockSpec((1, tm, k), lambda j, b, i: (b, i, 0)),
            pl.BlockSpec((k, tn), lambda j, b, i: (0, j)),
        ],
        out_specs=pl.BlockSpec((1, tm, tn), lambda j, b, i: (b, i, j)),
        out_shape=jax.ShapeDtypeStruct((nb, length, n), out_dtype),
        compiler_params=_params("arbitrary", "arbitrary", "arbitrary"),
        name="matmul",
    )(a, w)


def _shifted_taps(pad_ref, r0, rows, taps):
    half = taps // 2
    start = r0 + 16 - half
    return [pad_ref[start + k:start + k + rows, :] for k in range(taps)]


def _fill_padded(pad_ref, vals):
    t = vals.shape[0]
    zeros = jnp.zeros((16, vals.shape[1]), F32)
    pad_ref[0:16, :] = zeros
    pad_ref[t + 16:t + 32, :] = zeros
    pad_ref[16:t + 16, :] = vals


def _conv_xbc_kernel(x_ref, w_ref, b_ref, o_ref, pad_ref, *, rows):
    t = x_ref.shape[1]
    _fill_padded(pad_ref, x_ref[0].astype(F32))
    for r0 in range(0, t, rows):
        taps = _shifted_taps(pad_ref, r0, rows, SSD_CONV_W)
        acc = b_ref[...] + w_ref[0:1, :] * taps[0]
        for k in range(1, SSD_CONV_W):
            acc = acc + w_ref[k:k + 1, :] * taps[k]
        o_ref[0, r0:r0 + rows, :] = _silu(acc).astype(o_ref.dtype)


def conv_xbc(p, conv_w, conv_b, ct=256):
    nseq, t, _ = p.shape
    off = SSD_WIDTH // ct
    w8 = jnp.zeros((8, SSD_XBC), F32).at[:SSD_CONV_W].set(conv_w)
    return pl.pallas_call(
        functools.partial(_conv_xbc_kernel, rows=min(t, 256)),
        grid=(nseq, SSD_XBC // ct),
        in_specs=[
            pl.BlockSpec((1, t, ct), lambda s, j: (s, 0, off + j)),
            pl.BlockSpec((8, ct), lambda s, j: (0, j)),
            pl.BlockSpec((1, ct), lambda s, j: (0, j)),
        ],
        out_specs=pl.BlockSpec((1, t, ct), lambda s, j: (s, 0, j)),
        out_shape=jax.ShapeDtypeStruct((nseq, t, SSD_XBC), BF16),
        scratch_shapes=[pltpu.VMEM((t + 32, ct), F32)],
        compiler_params=_params("arbitrary", "arbitrary"),
        name="conv_xbc",
    )(p, w8, conv_b.reshape(1, SSD_XBC))


def _conv_sc_kernel(gb_ref, gc_ref, v_ref, w_ref, o_ref, pad_ref, *, rows):
    t = v_ref.shape[1]
    _fill_padded(pad_ref, gc_ref[0].astype(F32) * v_ref[0].astype(F32))
    for r0 in range(0, t, rows):
        taps = _shifted_taps(pad_ref, r0, rows, SC_CONV_W)
        acc = w_ref[0:1, :] * taps[0]
        for k in range(1, SC_CONV_W):
            acc = acc + w_ref[k:k + 1, :] * taps[k]
        gate = gb_ref[0, r0:r0 + rows, :].astype(F32)
        o_ref[0, r0:r0 + rows, :] = (gate * acc).astype(o_ref.dtype)


def conv_sc(p, sc_w, ct=256):
    nseq, t, _ = p.shape
    base = (SSD_WIDTH + SSD_XBC) // ct
    step = SC_WIDTH // ct
    w8 = jnp.zeros((8, SC_WIDTH), F32).at[:SC_CONV_W].set(sc_w)
    col = lambda k: pl.BlockSpec((1, t, ct), lambda s, j: (s, 0, base + k * step + j))
    return pl.pallas_call(
        functools.partial(_conv_sc_kernel, rows=min(t, 256)),
        grid=(nseq, SC_WIDTH // ct),
        in_specs=[col(0), col(1), col(2), pl.BlockSpec((8, ct), lambda s, j: (0, j))],
        out_specs=pl.BlockSpec((1, t, ct), lambda s, j: (s, 0, j)),
        out_shape=jax.ShapeDtypeStruct((nseq, t, SC_WIDTH), BF16),
        scratch_shapes=[pltpu.VMEM((t + 32, ct), F32)],
        compiler_params=_params("arbitrary", "arbitrary"),
        name="conv_sc",
    )(p, p, p, w8)


def _split3(v):
    hi = v.astype(BF16)
    r1 = v - hi.astype(F32)
    mid = r1.astype(BF16)
    return hi, mid, (r1 - mid.astype(F32)).astype(BF16)


def _cumsums(v, tri):
    return _dot(tri, jnp.concatenate(_split3(v), axis=0))


HEAD_DIRS = 2 * SSD_HEADS
SEL_ROWS = 2 * LANES
SEL_SPREAD = HEAD_DIRS * LANES
SEL_FWD = SEL_SPREAD + SSD_WIDTH
SEL_COLS = SEL_FWD + 2 * SSD_WIDTH


def _selection_matrix():
    s = np.zeros((SEL_ROWS, SEL_COLS), np.float32)
    for band in range(3):
        base = band * 2 * HEAD_DIRS
        for k in range(HEAD_DIRS):
            s[base + k, k * LANES:(k + 1) * LANES] = 1.0
        for h in range(SSD_HEADS):
            ch = slice(h * SSD_HEAD_DIM, (h + 1) * SSD_HEAD_DIM)
            s[base + HEAD_DIRS + h, SEL_SPREAD:SEL_FWD][ch] = 1.0
            s[base + SSD_HEADS + h, SEL_FWD:SEL_FWD + SSD_WIDTH][ch] = 1.0
            s[base + HEAD_DIRS + SSD_HEADS + h, SEL_FWD + SSD_WIDTH:SEL_COLS][ch] = 1.0
    return jnp.asarray(s, BF16)


def _packed_operand(acs, dt):
    lane = lax.broadcasted_iota(jnp.int32, acs.shape, 1)
    packed = jnp.where((lane & (2 * HEAD_DIRS - 1)) < HEAD_DIRS, acs, dt)
    hi, mid, lo = _split3(packed)
    first = jnp.where(lane < 2 * HEAD_DIRS, hi.astype(F32), mid.astype(F32)).astype(BF16)
    return jnp.concatenate([first, lo], axis=1)


def _state_update(st_ref, b_mat, xw, decay_row):
    for g in range(SSD_GROUPS):
        cols = slice(g * HEAD_GROUP_COLS, (g + 1) * HEAD_GROUP_COLS)
        contrib = lax.dot_general(
            b_mat[:, g * SSD_STATE:(g + 1) * SSD_STATE], xw[:, cols],
            (((0,), (0,)), ((), ())), preferred_element_type=F32)
        st_ref[:, cols] = st_ref[:, cols] * decay_row[:, cols] + contrib


def _ssd_kernel(xbc_ref, dt_ref, z_ref, h0f_ref, h0b_ref, dtb_ref, alog_ref, dsk_ref, g_ref, tri_ref,
                sel_ref, y_ref, hfl_ref, hbl_ref, hbe_ref, stf_ref, stb_ref, *, nc):
    q = SSD_CHUNK
    j = pl.program_id(1)
    fwd_lanes = 0
    bwd_lanes = SSD_HEADS

    dt = _softplus(dt_ref[0] + dtb_ref[...])
    d_a = dt * (-jnp.exp(alog_ref[...]) * LOG2_E)
    xs_bf = xbc_ref[0, :, 0:SSD_WIDTH]
    xs = xs_bf.astype(F32)
    b_mat = xbc_ref[0, :, SSD_WIDTH:SSD_WIDTH + SSD_GN]

    @pl.when(j == 0)
    def _():
        stf_ref[...] = h0f_ref[0]
        stb_ref[...] = h0b_ref[0]

    @pl.when(j < nc)
    def _backward_states():
        c = nc - 1 - j
        acs_b = _cumsums(d_a, tri_ref[q:2 * q, :])
        spread = _dot(_packed_operand(acs_b, dt), sel_ref[:, SEL_FWD:SEL_COLS])
        sum_b = spread[:, 0:SSD_WIDTH]
        tot = sum_b[0:1, :]
        w_full = spread[:, SSD_WIDTH:2 * SSD_WIDTH] * jnp.exp2(tot - sum_b)
        xw = (xs * w_full).astype(BF16)
        hbe_ref[c] = stb_ref[...].astype(BF16)
        _state_update(stb_ref, b_mat, xw, jnp.exp2(tot))

        @pl.when(j == nc - 1)
        def _():
            hbl_ref[0] = stb_ref[...]

    @pl.when(j >= nc)
    def _forward_outputs():
        c = j - nc
        row = lax.broadcasted_iota(jnp.int32, (q, q), 0)
        lane = lax.broadcasted_iota(jnp.int32, (q, q), 1)
        sums = _cumsums(d_a, tri_ref[...])
        acs = jnp.where((lane & (HEAD_DIRS - 1)) < bwd_lanes, sums[0:q], sums[q:2 * q])
        spread = _dot(_packed_operand(acs, dt), sel_ref[:, 0:SEL_FWD])
        acs_t = acs.T
        dt_t = dt.T
        src_t = acs_t - jnp.log2(dt_t)
        both_t = jnp.log2(dt_t[fwd_lanes:fwd_lanes + SSD_HEADS] + dt_t[bwd_lanes:bwd_lanes + SSD_HEADS])
        c_mat = xbc_ref[0, :, SSD_WIDTH + SSD_GN:SSD_XBC]
        below = row > lane
        diag = row == lane

        cb = []
        for g in range(SSD_GROUPS):
            gs = slice(g * SSD_STATE, (g + 1) * SSD_STATE)
            cb.append(lax.dot_general(c_mat[:, gs], b_mat[:, gs], (((1,), (1,)), ((), ())),
                                      preferred_element_type=F32))

        y_parts = []
        for i in range(SSD_HEADS // 2):
            mats = []
            for h in (2 * i, 2 * i + 1):
                hf, hb = fwd_lanes + h, bwd_lanes + h
                tgt = jnp.where(below, spread[:, hf * LANES:(hf + 1) * LANES],
                                spread[:, hb * LANES:(hb + 1) * LANES])
                src = jnp.where(below, src_t[hf:hf + 1, :], src_t[hb:hb + 1, :])
                arg = jnp.where(diag, both_t[h:h + 1, :], tgt - src)
                mats.append((cb[h // (SSD_HEADS // SSD_GROUPS)] * jnp.exp2(arg)).astype(BF16))
            pair = xs[:, i * LANES:(i + 1) * LANES]
            top = jnp.where(lane < SSD_HEAD_DIM, pair, 0.0).astype(BF16)
            bot = jnp.where(lane >= SSD_HEAD_DIM, pair, 0.0).astype(BF16)
            y_parts.append(_dot(jnp.concatenate(mats, axis=1), jnp.concatenate([top, bot], axis=0)))
        y = jnp.concatenate(y_parts, axis=1)

        half = (lax.broadcasted_iota(jnp.int32, (q, SSD_WIDTH), 1) & (LANES - 1)) < SSD_HEAD_DIM

        def per_channel(first_block):
            even = [spread[:, (first_block + 2 * i) * LANES:(first_block + 2 * i + 1) * LANES]
                    for i in range(SSD_HEADS // 2)]
            odd = [spread[:, (first_block + 2 * i + 1) * LANES:(first_block + 2 * i + 2) * LANES]
                   for i in range(SSD_HEADS // 2)]
            return jnp.where(half, jnp.concatenate(even, axis=1), jnp.concatenate(odd, axis=1))

        sum_f = per_channel(fwd_lanes)
        e_f = jnp.exp2(sum_f)
        e_b = jnp.exp2(per_channel(bwd_lanes))
        st_f = stf_ref[...].astype(BF16)
        st_b = hbe_ref[c]
        off_f, off_b = [], []
        for g in range(SSD_GROUPS):
            gs = slice(g * SSD_STATE, (g + 1) * SSD_STATE)
            cols = slice(g * HEAD_GROUP_COLS, (g + 1) * HEAD_GROUP_COLS)
            off_f.append(_dot(c_mat[:, gs], st_f[:, cols]))
            off_b.append(_dot(c_mat[:, gs], st_b[:, cols]))
        y = y + e_f * jnp.concatenate(off_f, axis=1) + e_b * jnp.concatenate(off_b, axis=1)
        y = y + dsk_ref[...] * xs

        w_full = spread[:, SEL_SPREAD:SEL_FWD] * jnp.exp2(sum_f[q - 1:q, :] - sum_f)
        _state_update(stf_ref, b_mat, (xs * w_full).astype(BF16), e_f[q - 1:q, :])

        z = z_ref[0].astype(F32)
        y_ref[0] = _norm_rows(y * _silu(z), g_ref[...]).astype(y_ref.dtype)

        @pl.when(j == 2 * nc - 1)
        def _():
            hfl_ref[0] = stf_ref[...]


def ssd(xbc, dt_raw, p, h0_f, h0_b, dt_bias, a_log, d_skip, norm_g):
    nseq, t, _ = xbc.shape
    nc = t // SSD_CHUNK
    pad_row = lambda v: jnp.tile(v.reshape(1, HEAD_DIRS), (1, LANES // HEAD_DIRS))
    ones = np.ones((SSD_CHUNK, SSD_CHUNK), np.float32)
    tri = jnp.asarray(np.concatenate([np.tile(np.tril(ones), (1, 3)), np.tile(np.triu(ones), (1, 3))]), BF16)
    chunk = lambda s, j: (s, jnp.where(j < nc, nc - 1 - j, j - nc), 0)
    out_chunk = lambda s, j: (s, jnp.maximum(j - nc, 0), 0)
    const2 = lambda s, j: (0, 0)
    seq = lambda s, j: (s, 0, 0)
    state_shape = jax.ShapeDtypeStruct((nseq, SSD_STATE, SSD_WIDTH), F32)
    return pl.pallas_call(
        functools.partial(_ssd_kernel, nc=nc),
        grid=(nseq, 2 * nc),
        in_specs=[
            pl.BlockSpec((1, SSD_CHUNK, SSD_XBC), chunk),
            pl.BlockSpec((1, SSD_CHUNK, LANES), chunk),
            pl.BlockSpec((1, SSD_CHUNK, SSD_WIDTH), out_chunk),
            pl.BlockSpec((1, SSD_STATE, SSD_WIDTH), seq),
            pl.BlockSpec((1, SSD_STATE, SSD_WIDTH), seq),
            pl.BlockSpec((1, LANES), const2),
            pl.BlockSpec((1, LANES), const2),
            pl.BlockSpec((1, SSD_WIDTH), const2),
            pl.BlockSpec((1, SSD_WIDTH), const2),
            pl.BlockSpec((2 * SSD_CHUNK, 3 * SSD_CHUNK), const2),
            pl.BlockSpec((SEL_ROWS, SEL_COLS), const2),
        ],
        out_specs=[
            pl.BlockSpec((1, SSD_CHUNK, SSD_WIDTH), out_chunk),
            pl.BlockSpec((1, SSD_STATE, SSD_WIDTH), seq),
            pl.BlockSpec((1, SSD_STATE, SSD_WIDTH), seq),
        ],
        out_shape=[jax.ShapeDtypeStruct((nseq, t, SSD_WIDTH), BF16), state_shape, state_shape],
        scratch_shapes=[
            pltpu.VMEM((nc, SSD_STATE, SSD_WIDTH), BF16),
            pltpu.VMEM((SSD_STATE, SSD_WIDTH), F32),
            pltpu.VMEM((SSD_STATE, SSD_WIDTH), F32),
        ],
        compiler_params=_params("arbitrary", "arbitrary"),
        name="ssd",
    )(xbc, dt_raw, p, h0_f, h0_b, pad_row(dt_bias), pad_row(a_log),
      jnp.repeat(d_skip, SSD_HEAD_DIM).reshape(1, SSD_WIDTH), norm_g.reshape(1, SSD_WIDTH), tri,
      _selection_matrix())


def _out_proj_kernel(y1_ref, y2_ref, w1_ref, w2_ref, x_ref, gate_ref, g_ref, sh_ref, sc_ref,
                     xo_ref, h_ref, *, sub):
    y1 = y1_ref[0]
    y2 = y2_ref[0]
    n = xo_ref.shape[-1]
    for n0 in range(0, n, sub):
        ns = slice(n0, n0 + sub)
        acc = _dot(y1, w1_ref[:, ns]) + _dot(y2, w2_ref[:, ns])
        xo_ref[0, :, ns] = x_ref[0, :, ns] + gate_ref[0, :, ns] * acc
    hn = _norm_rows(xo_ref[0], g_ref[...])
    h_ref[0] = (hn * (1.0 + sc_ref[0]) + sh_ref[0]).astype(h_ref.dtype)


def out_proj(y_ssd, y_sc, w1, w2, x, gate, g, shift, scale, tm=512, sub=1024):
    nb, length, d = x.shape
    k1, k2 = y_ssd.shape[-1], y_sc.shape[-1]
    tok = lambda b, i: (b, i, 0)
    per_b = lambda b, i: (b, 0, 0)
    return pl.pallas_call(
        functools.partial(_out_proj_kernel, sub=sub),
        grid=(nb, length // tm),
        in_specs=[
            pl.BlockSpec((1, tm, k1), tok),
            pl.BlockSpec((1, tm, k2), tok),
            pl.BlockSpec((k1, d), lambda b, i: (0, 0)),
            pl.BlockSpec((k2, d), lambda b, i: (0, 0)),
            pl.BlockSpec((1, tm, d), tok),
            pl.BlockSpec((1, 1, d), per_b),
            pl.BlockSpec((1, d), lambda b, i: (0, 0)),
            pl.BlockSpec((1, 1, d), per_b),
            pl.BlockSpec((1, 1, d), per_b),
        ],
        out_specs=[pl.BlockSpec((1, tm, d), tok), pl.BlockSpec((1, tm, d), tok)],
        out_shape=[jax.ShapeDtypeStruct((nb, length, d), F32),
                   jax.ShapeDtypeStruct((nb, length, d), BF16)],
        compiler_params=_params("arbitrary", "arbitrary"),
        name="out_proj",
    )(y_ssd, y_sc, w1, w2, x, gate, g.reshape(1, d), shift, scale)


def _ffn_up_kernel(h_ref, wg_ref, wu_ref, o_ref, wgb_ref, wub_ref, *, sub_m):
    @pl.when((pl.program_id(1) == 0) & (pl.program_id(2) == 0))
    def _():
        wgb_ref[...] = wg_ref[0].astype(BF16)
        wub_ref[...] = wu_ref[0].astype(BF16)

    tm = h_ref.shape[1]
    for m0 in range(0, tm, sub_m):
        h = h_ref[0, m0:m0 + sub_m, :]
        o_ref[0, m0:m0 + sub_m, :] = (_silu(_dot(h, wgb_ref[...])) * _dot(h, wub_ref[...])).astype(o_ref.dtype)


def ffn_up(h, w_gate, w_up, layer, tm=2048, tn=512, sub_m=512):
    nb, length, d = h.shape
    n = w_gate.shape[-1]
    tm = min(tm, length)
    w_spec = pl.BlockSpec((1, d, tn), lambda j, b, i: (layer, 0, j))
    return pl.pallas_call(
        functools.partial(_ffn_up_kernel, sub_m=sub_m),
        grid=(n // tn, nb, length // tm),
        in_specs=[pl.BlockSpec((1, tm, d), lambda j, b, i: (b, i, 0)), w_spec, w_spec],
        out_specs=pl.BlockSpec((1, tm, tn), lambda j, b, i: (b, i, j)),
        out_shape=jax.ShapeDtypeStruct((nb, length, n), BF16),
        scratch_shapes=[pltpu.VMEM((d, tn), BF16), pltpu.VMEM((d, tn), BF16)],
        compiler_params=_params("arbitrary", "arbitrary", "arbitrary"),
        name="ffn_up",
    )(h, w_gate, w_up)


def _ffn_down_kernel(a_ref, w_ref, x_ref, gate_ref, g_ref, sh_ref, sc_ref, *refs, sub_n):
    res_ref, h_ref = refs[0], refs[-1]
    n = res_ref.shape[2]
    a = a_ref[0]
    for n0 in range(0, n, sub_n):
        ns = slice(n0, n0 + sub_n)
        res_ref[0, :, ns] = x_ref[0, :, ns] + gate_ref[0, :, ns] * _dot(a, w_ref[:, ns])
    hn = _norm_rows(res_ref[0], g_ref[...])
    h_ref[0] = (hn * (1.0 + sc_ref[0]) + sh_ref[0]).astype(h_ref.dtype)


def ffn_down(act, w_down, x, gate, g, shift, scale, emit_x, h_dtype, tm=256):
    nb, length, d = x.shape
    kk = act.shape[-1]
    tok = lambda b, i: (b, i, 0)
    per_b = lambda b, i: (b, 0, 0)
    if emit_x:
        out_shape = [jax.ShapeDtypeStruct((nb, length, d), F32), jax.ShapeDtypeStruct((nb, length, d), h_dtype)]
    else:
        assert h_dtype == F32
        out_shape = [jax.ShapeDtypeStruct((nb, length, d), F32)]
    return pl.pallas_call(
        functools.partial(_ffn_down_kernel, sub_n=d // 2),
        grid=(nb, length // tm),
        in_specs=[
            pl.BlockSpec((1, tm, kk), tok),
            pl.BlockSpec((kk, d), lambda b, i: (0, 0), pipeline_mode=pl.Buffered(1)),
            pl.BlockSpec((1, tm, d), tok),
            pl.BlockSpec((1, 1, d), per_b),
            pl.BlockSpec((1, d), lambda b, i: (0, 0)),
            pl.BlockSpec((1, 1, d), per_b),
            pl.BlockSpec((1, 1, d), per_b),
        ],
        out_specs=[pl.BlockSpec((1, tm, d), tok)] * len(out_shape),
        out_shape=out_shape,
        compiler_params=_params("arbitrary", "arbitrary"),
        name="ffn_down",
    )(act, w_down, x, gate, g.reshape(1, d), shift, scale)


def _to_col_major(t, rows):
    b, length, ch = t.shape
    return t.reshape(b, rows, GRID_W, ch).transpose(0, 2, 1, 3).reshape(b, length, ch)


def _from_col_major(t, rows):
    b, length, ch = t.shape
    return t.reshape(b, GRID_W, rows, ch).transpose(0, 2, 1, 3).reshape(b, length, ch)


def _mixer(h_rows, nseq, w_main, w_dt, layer, h0_f, h0_b):
    (conv_w, conv_b, dt_bias, a_log, d_skip, ssd_g, sc_w) = layer
    nb, length, _ = h_rows.shape
    t = nb * length // nseq
    tm = min(length, 1024)
    n_main = w_main.shape[1]
    p = matmul(h_rows, w_main, BF16, tm=tm, tn=n_main // 2, sub=256)
    dt_raw = matmul(h_rows, w_dt, F32, tm=tm, tn=LANES, sub=LANES)
    p = p.reshape(nseq, t, n_main)
    dt_raw = dt_raw.reshape(nseq, t, LANES)
    xbc = conv_xbc(p, conv_w, conv_b)
    y_ssd, h_f, h_b = ssd(xbc, dt_raw, p, h0_f, h0_b, dt_bias, a_log, d_skip, ssd_g)
    y_sc = conv_sc(p, sc_w) if n_main == MAIN_COLS else None
    return y_ssd, y_sc, h_f, h_b


def kernel(x, c, ctx, c_ctx, ada_w, ada_b, mix_norm_g, w_in, ssd_conv_w, ssd_conv_b, ssd_dt_bias,
           ssd_a_log, ssd_d, ssd_norm_g, sc_conv_w, w_out, ffn_norm_g, w_gate, w_up, w_down,
           final_norm_g):
    batch, seq, d = x.shape
    depth = ada_w.shape[0]
    ctx_len = ctx.shape[1]
    rows = seq // GRID_W
    n_ctx_rows = batch * ctx_len

    cc = jnp.zeros((16, d), F32).at[:batch].set(c).at[batch].set(c_ctx)
    mods = adaln(cc, ada_w, ada_b)

    def mod_vectors(l):
        mx = [mods[l, :batch, i * d:(i + 1) * d].reshape(batch, 1, d) for i in range(6)]
        mc = [mods[l, batch:batch + 1, i * d:(i + 1) * d].reshape(1, 1, d) for i in range(6)]
        return mx, mc

    dt_lo = SSD_WIDTH + SSD_XBC
    w_main = [jnp.concatenate([w_in[l][:, :dt_lo], w_in[l][:, SSD_COLS:]], axis=1).astype(BF16)
              for l in range(depth)]
    w_dt = [jnp.tile(w_in[l][:, dt_lo:SSD_COLS].astype(BF16), (1, LANES // HEAD_DIRS))
            for l in range(depth)]
    w_out_b = w_out.astype(BF16)
    w_down_b = w_down.astype(BF16)

    h_ctx = ctx.reshape(1, n_ctx_rows, d)
    mx, mc = mod_vectors(0)
    hx = norm_mod(x, mix_norm_g[0], mx[0], mx[1])
    hc = norm_mod(h_ctx, mix_norm_g[0], mc[0], mc[1])
    zero_state = jnp.zeros((batch, SSD_STATE, SSD_WIDTH), F32)
    zeros_d = jnp.zeros((batch, 1, d), F32)
    out = None

    for l in range(depth):
        last = l == depth - 1
        layer = (ssd_conv_w[l], ssd_conv_b[l], ssd_dt_bias[l], ssd_a_log[l], ssd_d[l],
                 ssd_norm_g[l], sc_conv_w[l])
        w_o1, w_o2 = w_out_b[l, :SSD_WIDTH], w_out_b[l, SSD_WIDTH:]
        if not last:
            mx_next, mc_next = mod_vectors(l + 1)

        if last:
            _, _, state_f, state_b = _mixer(hc, batch, w_main[l][:, :dt_lo], w_dt[l], layer,
                                            zero_state, zero_state)
        else:
            y_ssd, y_sc, state_f, state_b = _mixer(hc, batch, w_main[l], w_dt[l], layer,
                                                   zero_state, zero_state)
            h_ctx, hf = out_proj(y_ssd.reshape(1, n_ctx_rows, -1), y_sc.reshape(1, n_ctx_rows, -1),
                                 w_o1, w_o2, h_ctx, mc[2], ffn_norm_g[l], mc[3], mc[4])
            act = ffn_up(hf, w_gate, w_up, l)
            h_ctx, hc = ffn_down(act, w_down_b[l], h_ctx, mc[5], mix_norm_g[l + 1],
                                 mc_next[0], mc_next[1], emit_x=True, h_dtype=BF16)

        col_major = l % 2 == 1
        if col_major:
            hx = _to_col_major(hx, rows)
        y_ssd, y_sc, _, _ = _mixer(hx, batch, w_main[l], w_dt[l], layer, state_f, state_b)
        if col_major:
            y_ssd = _from_col_major(y_ssd, rows)
            y_sc = _from_col_major(y_sc, rows)
        x, hf = out_proj(y_ssd, y_sc, w_o1, w_o2, x, mx[2], ffn_norm_g[l], mx[3], mx[4])
        act = ffn_up(hf, w_gate, w_up, l)
        if last:
            (out,) = ffn_down(act, w_down_b[l], x, mx[5], final_norm_g, zeros_d, zeros_d,
                              emit_x=False, h_dtype=x.dtype)
        else:
            x, hx = ffn_down(act, w_down_b[l], x, mx[5], mix_norm_g[l + 1],
                             mx_next[0], mx_next[1], emit_x=True, h_dtype=BF16)
            mx, mc = mx_next, mc_next
    return out
```

```python
import functools

import jax
import jax.numpy as jnp
import numpy as np
from jax import lax
from jax.experimental import pallas as pl
from jax.experimental.pallas import tpu as pltpu

F32 = jnp.float32
BF16 = jnp.bfloat16

D_MODEL = 2048
GRID_W = 64
NORM_EPS = 1e-6
SSD_WIDTH = 1024
SSD_HEAD_DIM = 64
SSD_HEADS = 16
SSD_GROUPS = 2
SSD_STATE = 128
SSD_CONV_W = 5
SSD_CHUNK = 128
SSD_GN = SSD_GROUPS * SSD_STATE
SSD_XBC = SSD_WIDTH + 2 * SSD_GN
SSD_COLS = SSD_WIDTH + SSD_XBC + 2 * SSD_HEADS
SC_WIDTH = 1024
SC_CONV_W = 3
MAIN_COLS = SSD_WIDTH + SSD_XBC + 3 * SC_WIDTH
FFN_HIDDEN = 5632
LANES = 128
HEAD_GROUP_COLS = SSD_WIDTH // SSD_GROUPS
VMEM_LIMIT = 56 * 1024 * 1024
FFN_DOWN_VMEM_LIMIT = 62 * 1024 * 1024
LOG2_E = 1.4426950408889634


def _params(*sem, vmem=VMEM_LIMIT):
    return pltpu.CompilerParams(dimension_semantics=sem, vmem_limit_bytes=vmem)


def _silu(v):
    return v * jax.nn.sigmoid(v)


def _softplus(v):
    return jnp.maximum(v, 0.0) + jnp.log1p(jnp.exp(-jnp.abs(v)))


def _norm_rows(v, g):
    ms = jnp.mean(v * v, axis=-1, keepdims=True)
    return v * lax.rsqrt(ms + NORM_EPS) * g


def _dot(a, b):
    return jnp.dot(a, b, preferred_element_type=F32)


def _adaln_kernel(c_ref, w_ref, b_ref, o_ref):
    s = _silu(c_ref[...]).astype(BF16)
    o_ref[0] = _dot(s, w_ref[0].astype(BF16)) + b_ref[0]


def adaln(cc, ada_w, ada_b, tn=1536):
    depth, d, n = ada_w.shape
    rows = cc.shape[0]
    return pl.pallas_call(
        _adaln_kernel,
        grid=(depth, n // tn),
        in_specs=[
            pl.BlockSpec((rows, d), lambda l, j: (0, 0)),
            pl.BlockSpec((1, d, tn), lambda l, j: (l, 0, j)),
            pl.BlockSpec((1, 1, tn), lambda l, j: (l, 0, j)),
        ],
        out_specs=pl.BlockSpec((1, rows, tn), lambda l, j: (l, 0, j)),
        out_shape=jax.ShapeDtypeStruct((depth, rows, n), F32),
        compiler_params=_params("arbitrary", "arbitrary"),
        name="adaln",
    )(cc, ada_w, ada_b.reshape(depth, 1, n))


def _norm_mod_kernel(x_ref, g_ref, sh_ref, sc_ref, o_ref):
    y = _norm_rows(x_ref[0], g_ref[...])
    o_ref[0] = (y * (1.0 + sc_ref[0]) + sh_ref[0]).astype(o_ref.dtype)


def norm_mod(x, g, shift, scale, tr=512):
    nb, length, d = x.shape
    return pl.pallas_call(
        _norm_mod_kernel,
        grid=(nb, length // tr),
        in_specs=[
            pl.BlockSpec((1, tr, d), lambda b, i: (b, i, 0)),
            pl.BlockSpec((1, d), lambda b, i: (0, 0)),
            pl.BlockSpec((1, 1, d), lambda b, i: (b, 0, 0)),
            pl.BlockSpec((1, 1, d), lambda b, i: (b, 0, 0)),
        ],
        out_specs=pl.BlockSpec((1, tr, d), lambda b, i: (b, i, 0)),
        out_shape=jax.ShapeDtypeStruct((nb, length, d), BF16),
        compiler_params=_params("arbitrary", "arbitrary"),
        name="norm_mod",
    )(x, g.reshape(1, d), shift, scale)


def _matmul_kernel(a_ref, w_ref, o_ref, *, sub):
    a = a_ref[0]
    tn = o_ref.shape[-1]
    for n0 in range(0, tn, sub):
        o_ref[0, :, n0:n0 + sub] = _dot(a, w_ref[:, n0:n0 + sub]).astype(o_ref.dtype)


def matmul(a, w, n, out_dtype, tm, tn, sub):
    nb, length, k = a.shape
    return pl.pallas_call(
        functools.partial(_matmul_kernel, sub=sub),
        grid=(n // tn, nb, length // tm),
        in_specs=[
            pl.BlockSpec((1, tm, k), lambda j, b, i: (b, i, 0)),
            pl.BlockSpec((k, tn), lambda j, b, i: (0, j)),
        ],
        out_specs=pl.BlockSpec((1, tm, tn), lambda j, b, i: (b, i, j)),
        out_shape=jax.ShapeDtypeStruct((nb, length, n), out_dtype),
        compiler_params=_params("arbitrary", "arbitrary", "arbitrary"),
        name="matmul",
    )(a, w)


def _shifted_taps(pad_ref, r0, rows, taps):
    half = taps // 2
    win = pad_ref[r0:r0 + rows + 32, :]
    out = []
    for k in range(taps):
        shift = (half - k) % (rows + 32)
        rolled = win if shift == 0 else pltpu.roll(win, shift, axis=0)
        out.append(rolled[16:16 + rows, :])
    return out


def _fill_padded(pad_ref, vals):
    t = vals.shape[0]
    zeros = jnp.zeros((16, vals.shape[1]), F32)
    pad_ref[0:16, :] = zeros
    pad_ref[t + 16:t + 32, :] = zeros
    pad_ref[16:t + 16, :] = vals


def _conv_xbc_kernel(x_ref, w_ref, b_ref, o_ref, pad_ref, *, rows):
    t = x_ref.shape[1]
    _fill_padded(pad_ref, x_ref[0].astype(F32))
    for r0 in range(0, t, rows):
        taps = _shifted_taps(pad_ref, r0, rows, SSD_CONV_W)
        acc = b_ref[...] + w_ref[0:1, :] * taps[0]
        for k in range(1, SSD_CONV_W):
            acc = acc + w_ref[k:k + 1, :] * taps[k]
        o_ref[0, r0:r0 + rows, :] = _silu(acc).astype(o_ref.dtype)


def conv_xbc(p, conv_w, conv_b, ct=256):
    nseq, t, _ = p.shape
    off = SSD_WIDTH // ct
    w8 = jnp.zeros((8, SSD_XBC), F32).at[:SSD_CONV_W].set(conv_w)
    return pl.pallas_call(
        functools.partial(_conv_xbc_kernel, rows=min(t, 256)),
        grid=(nseq, SSD_XBC // ct),
        in_specs=[
            pl.BlockSpec((1, t, ct), lambda s, j: (s, 0, off + j)),
            pl.BlockSpec((8, ct), lambda s, j: (0, j)),
            pl.BlockSpec((1, ct), lambda s, j: (0, j)),
        ],
        out_specs=pl.BlockSpec((1, t, ct), lambda s, j: (s, 0, j)),
        out_shape=jax.ShapeDtypeStruct((nseq, t, SSD_XBC), BF16),
        scratch_shapes=[pltpu.VMEM((t + 32, ct), F32)],
        compiler_params=_params("arbitrary", "arbitrary"),
        name="conv_xbc",
    )(p, w8, conv_b.reshape(1, SSD_XBC))


def _conv_sc_kernel(gb_ref, gc_ref, v_ref, w_ref, o_ref, pad_ref, *, rows):
    t = v_ref.shape[1]
    _fill_padded(pad_ref, gc_ref[0].astype(F32) * v_ref[0].astype(F32))
    for r0 in range(0, t, rows):
        taps = _shifted_taps(pad_ref, r0, rows, SC_CONV_W)
        acc = w_ref[0:1, :] * taps[0]
        for k in range(1, SC_CONV_W):
            acc = acc + w_ref[k:k + 1, :] * taps[k]
        gate = gb_ref[0, r0:r0 + rows, :].astype(F32)
        o_ref[0, r0:r0 + rows, :] = (gate * acc).astype(o_ref.dtype)


def conv_sc(p, sc_w, ct=256):
    nseq, t, _ = p.shape
    base = (SSD_WIDTH + SSD_XBC) // ct
    step = SC_WIDTH // ct
    w8 = jnp.zeros((8, SC_WIDTH), F32).at[:SC_CONV_W].set(sc_w)
    col = lambda k: pl.BlockSpec((1, t, ct), lambda s, j: (s, 0, base + k * step + j))
    return pl.pallas_call(
        functools.partial(_conv_sc_kernel, rows=min(t, 256)),
        grid=(nseq, SC_WIDTH // ct),
        in_specs=[col(0), col(1), col(2), pl.BlockSpec((8, ct), lambda s, j: (0, j))],
        out_specs=pl.BlockSpec((1, t, ct), lambda s, j: (s, 0, j)),
        out_shape=jax.ShapeDtypeStruct((nseq, t, SC_WIDTH), BF16),
        scratch_shapes=[pltpu.VMEM((t + 32, ct), F32)],
        compiler_params=_params("arbitrary", "arbitrary"),
        name="conv_sc",
    )(p, p, p, w8)


def _split3(v):
    hi = v.astype(BF16)
    r1 = v - hi.astype(F32)
    mid = r1.astype(BF16)
    return hi, mid, (r1 - mid.astype(F32)).astype(BF16)


def _cumsums(v, tri):
    return _dot(tri, jnp.concatenate(_split3(v), axis=0))


HEAD_DIRS = 2 * SSD_HEADS
SEL_ROWS = 2 * LANES
SEL_SPREAD = HEAD_DIRS * LANES
SEL_FWD = SEL_SPREAD + SSD_WIDTH
SEL_COLS = SEL_FWD + 2 * SSD_WIDTH


def _selection_matrix():
    s = np.zeros((SEL_ROWS, SEL_COLS), np.float32)
    for band in range(3):
        base = band * 2 * HEAD_DIRS
        for k in range(HEAD_DIRS):
            s[base + k, k * LANES:(k + 1) * LANES] = 1.0
        for h in range(SSD_HEADS):
            ch = slice(h * SSD_HEAD_DIM, (h + 1) * SSD_HEAD_DIM)
            s[base + HEAD_DIRS + h, SEL_SPREAD:SEL_FWD][ch] = 1.0
            s[base + SSD_HEADS + h, SEL_FWD:SEL_FWD + SSD_WIDTH][ch] = 1.0
            s[base + HEAD_DIRS + SSD_HEADS + h, SEL_FWD + SSD_WIDTH:SEL_COLS][ch] = 1.0
    return jnp.asarray(s, BF16)


def _packed_operand(acs, dt):
    lane = lax.broadcasted_iota(jnp.int32, acs.shape, 1)
    packed = jnp.where((lane & (2 * HEAD_DIRS - 1)) < HEAD_DIRS, acs, dt)
    hi, mid, lo = _split3(packed)
    first = jnp.where(lane < 2 * HEAD_DIRS, hi.astype(F32), mid.astype(F32)).astype(BF16)
    return jnp.concatenate([first, lo], axis=1)


def _state_update(st_ref, b_mat, xw, decay_row):
    for g in range(SSD_GROUPS):
        cols = slice(g * HEAD_GROUP_COLS, (g + 1) * HEAD_GROUP_COLS)
        contrib = lax.dot_general(
            b_mat[:, g * SSD_STATE:(g + 1) * SSD_STATE], xw[:, cols],
            (((0,), (0,)), ((), ())), preferred_element_type=F32)
        st_ref[:, cols] = st_ref[:, cols] * decay_row[:, cols] + contrib


def _ssd_kernel(xbc_ref, dt_ref, z_ref, h0f_ref, h0b_ref, dtb_ref, alog_ref, dsk_ref, g_ref, tri_ref,
                sel_ref, y_ref, hfl_ref, hbl_ref, hbe_ref, stf_ref, stb_ref, *, nc):
    q = SSD_CHUNK
    j = pl.program_id(1)
    fwd_lanes = 0
    bwd_lanes = SSD_HEADS

    dt = _softplus(dt_ref[0] + dtb_ref[...])
    d_a = dt * (-jnp.exp(alog_ref[...]) * LOG2_E)
    xs_bf = xbc_ref[0, :, 0:SSD_WIDTH]
    xs = xs_bf.astype(F32)
    b_mat = xbc_ref[0, :, SSD_WIDTH:SSD_WIDTH + SSD_GN]

    @pl.when(j == 0)
    def _():
        stf_ref[...] = h0f_ref[0]
        stb_ref[...] = h0b_ref[0]

    @pl.when(j < nc)
    def _backward_states():
        c = nc - 1 - j
        acs_b = _cumsums(d_a, tri_ref[q:2 * q, :])
        spread = _dot(_packed_operand(acs_b, dt), sel_ref[:, SEL_FWD:SEL_COLS])
        sum_b = spread[:, 0:SSD_WIDTH]
        tot = sum_b[0:1, :]
        w_full = spread[:, SSD_WIDTH:2 * SSD_WIDTH] * jnp.exp2(tot - sum_b)
        xw = (xs * w_full).astype(BF16)
        hbe_ref[c] = stb_ref[...].astype(BF16)
        _state_update(stb_ref, b_mat, xw, jnp.exp2(tot))

        @pl.when(j == nc - 1)
        def _():
            hbl_ref[0] = stb_ref[...]

    @pl.when(j >= nc)
    def _forward_outputs():
        c = j - nc
        row = lax.broadcasted_iota(jnp.int32, (q, q), 0)
        lane = lax.broadcasted_iota(jnp.int32, (q, q), 1)
        sums = _cumsums(d_a, tri_ref[...])
        acs = jnp.where((lane & (HEAD_DIRS - 1)) < bwd_lanes, sums[0:q], sums[q:2 * q])
        spread = _dot(_packed_operand(acs, dt), sel_ref[:, 0:SEL_FWD])
        acs_t = acs.T
        dt_t = dt.T
        src_t = acs_t - jnp.log2(dt_t)
        both_t = jnp.log2(dt_t[fwd_lanes:fwd_lanes + SSD_HEADS] + dt_t[bwd_lanes:bwd_lanes + SSD_HEADS])
        c_mat = xbc_ref[0, :, SSD_WIDTH + SSD_GN:SSD_XBC]
        below = row > lane
        diag = row == lane

        cb = []
        for g in range(SSD_GROUPS):
            gs = slice(g * SSD_STATE, (g + 1) * SSD_STATE)
            cb.append(lax.dot_general(c_mat[:, gs], b_mat[:, gs], (((1,), (1,)), ((), ())),
                                      preferred_element_type=F32))

        y_parts = []
        for i in range(SSD_HEADS // 2):
            mats = []
            for h in (2 * i, 2 * i + 1):
                hf, hb = fwd_lanes + h, bwd_lanes + h
                tgt = jnp.where(below, spread[:, hf * LANES:(hf + 1) * LANES],
                                spread[:, hb * LANES:(hb + 1) * LANES])
                src = jnp.where(below, src_t[hf:hf + 1, :], src_t[hb:hb + 1, :])
                arg = jnp.where(diag, both_t[h:h + 1, :], tgt - src)
                mats.append((cb[h // (SSD_HEADS // SSD_GROUPS)] * jnp.exp2(arg)).astype(BF16))
            pair = xs[:, i * LANES:(i + 1) * LANES]
            top = jnp.where(lane < SSD_HEAD_DIM, pair, 0.0).astype(BF16)
            bot = jnp.where(lane >= SSD_HEAD_DIM, pair, 0.0).astype(BF16)
            y_parts.append(_dot(jnp.concatenate(mats, axis=1), jnp.concatenate([top, bot], axis=0)))
        y = jnp.concatenate(y_parts, axis=1)

        half = (lax.broadcasted_iota(jnp.int32, (q, SSD_WIDTH), 1) & (LANES - 1)) < SSD_HEAD_DIM

        def per_channel(first_block):
            even = [spread[:, (first_block + 2 * i) * LANES:(first_block + 2 * i + 1) * LANES]
                    for i in range(SSD_HEADS // 2)]
            odd = [spread[:, (first_block + 2 * i + 1) * LANES:(first_block + 2 * i + 2) * LANES]
                   for i in range(SSD_HEADS // 2)]
            return jnp.where(half, jnp.concatenate(even, axis=1), jnp.concatenate(odd, axis=1))

        sum_f = per_channel(fwd_lanes)
        e_f = jnp.exp2(sum_f)
        e_b = jnp.exp2(per_channel(bwd_lanes))
        st_f = stf_ref[...].astype(BF16)
        st_b = hbe_ref[c]
        off_f, off_b = [], []
        for g in range(SSD_GROUPS):
            gs = slice(g * SSD_STATE, (g + 1) * SSD_STATE)
            cols = slice(g * HEAD_GROUP_COLS, (g + 1) * HEAD_GROUP_COLS)
            off_f.append(_dot(c_mat[:, gs], st_f[:, cols]))
            off_b.append(_dot(c_mat[:, gs], st_b[:, cols]))
        y = y + e_f * jnp.concatenate(off_f, axis=1) + e_b * jnp.concatenate(off_b, axis=1)
        y = y + dsk_ref[...] * xs

        w_full = spread[:, SEL_SPREAD:SEL_FWD] * jnp.exp2(sum_f[q - 1:q, :] - sum_f)
        _state_update(stf_ref, b_mat, (xs * w_full).astype(BF16), e_f[q - 1:q, :])

        z = z_ref[0].astype(F32)
        y_ref[0] = _norm_rows(y * _silu(z), g_ref[...]).astype(y_ref.dtype)

        @pl.when(j == 2 * nc - 1)
        def _():
            hfl_ref[0] = stf_ref[...]


def ssd(xbc, dt_raw, p, h0_f, h0_b, dt_bias, a_log, d_skip, norm_g):
    nseq, t, _ = xbc.shape
    nc = t // SSD_CHUNK
    pad_row = lambda v: jnp.tile(v.reshape(1, HEAD_DIRS), (1, LANES // HEAD_DIRS))
    ones = np.ones((SSD_CHUNK, SSD_CHUNK), np.float32)
    tri = jnp.asarray(np.concatenate([np.tile(np.tril(ones), (1, 3)), np.tile(np.triu(ones), (1, 3))]), BF16)
    chunk = lambda s, j: (s, jnp.where(j < nc, nc - 1 - j, j - nc), 0)
    out_chunk = lambda s, j: (s, jnp.maximum(j - nc, 0), 0)
    const2 = lambda s, j: (0, 0)
    seq = lambda s, j: (s, 0, 0)
    state_shape = jax.ShapeDtypeStruct((nseq, SSD_STATE, SSD_WIDTH), F32)
    return pl.pallas_call(
        functools.partial(_ssd_kernel, nc=nc),
        grid=(nseq, 2 * nc),
        in_specs=[
            pl.BlockSpec((1, SSD_CHUNK, SSD_XBC), chunk),
            pl.BlockSpec((1, SSD_CHUNK, LANES), chunk),
            pl.BlockSpec((1, SSD_CHUNK, SSD_WIDTH), out_chunk),
            pl.BlockSpec((1, SSD_STATE, SSD_WIDTH), seq),
            pl.BlockSpec((1, SSD_STATE, SSD_WIDTH), seq),
            pl.BlockSpec((1, LANES), const2),
            pl.BlockSpec((1, LANES), const2),
            pl.BlockSpec((1, SSD_WIDTH), const2),
            pl.BlockSpec((1, SSD_WIDTH), const2),
            pl.BlockSpec((2 * SSD_CHUNK, 3 * SSD_CHUNK), const2),
            pl.BlockSpec((SEL_ROWS, SEL_COLS), const2),
        ],
        out_specs=[
            pl.BlockSpec((1, SSD_CHUNK, SSD_WIDTH), out_chunk),
            pl.BlockSpec((1, SSD_STATE, SSD_WIDTH), seq),
            pl.BlockSpec((1, SSD_STATE, SSD_WIDTH), seq),
        ],
        out_shape=[jax.ShapeDtypeStruct((nseq, t, SSD_WIDTH), BF16), state_shape, state_shape],
        scratch_shapes=[
            pltpu.VMEM((nc, SSD_STATE, SSD_WIDTH), BF16),
            pltpu.VMEM((SSD_STATE, SSD_WIDTH), F32),
            pltpu.VMEM((SSD_STATE, SSD_WIDTH), F32),
        ],
        compiler_params=_params("arbitrary", "arbitrary"),
        name="ssd",
    )(xbc, dt_raw, p, h0_f, h0_b, pad_row(dt_bias), pad_row(a_log),
      jnp.repeat(d_skip, SSD_HEAD_DIM).reshape(1, SSD_WIDTH), norm_g.reshape(1, SSD_WIDTH), tri,
      _selection_matrix())


def _out_proj_kernel(y1_ref, y2_ref, w_ref, x_ref, gate_ref, g_ref, sh_ref, sc_ref,
                     xo_ref, h_ref, *, sub):
    y = jnp.concatenate([y1_ref[0], y2_ref[0]], axis=1)
    n = xo_ref.shape[-1]
    for n0 in range(0, n, sub):
        ns = slice(n0, n0 + sub)
        xo_ref[0, :, ns] = x_ref[0, :, ns] + gate_ref[0, :, ns] * _dot(y, w_ref[0, :, ns])
    hn = _norm_rows(xo_ref[0], g_ref[...])
    h_ref[0] = (hn * (1.0 + sc_ref[0]) + sh_ref[0]).astype(h_ref.dtype)


def out_proj(y_ssd, y_sc, w_out, layer, x, gate, g, shift, scale, tm=512, sub=1024):
    nb, length, d = x.shape
    k1, k2 = y_ssd.shape[-1], y_sc.shape[-1]
    tok = lambda b, i: (b, i, 0)
    per_b = lambda b, i: (b, 0, 0)
    return pl.pallas_call(
        functools.partial(_out_proj_kernel, sub=sub),
        grid=(nb, length // tm),
        in_specs=[
            pl.BlockSpec((1, tm, k1), tok),
            pl.BlockSpec((1, tm, k2), tok),
            pl.BlockSpec((1, k1 + k2, d), lambda b, i: (layer, 0, 0)),
            pl.BlockSpec((1, tm, d), tok),
            pl.BlockSpec((1, 1, d), per_b),
            pl.BlockSpec((1, d), lambda b, i: (0, 0)),
            pl.BlockSpec((1, 1, d), per_b),
            pl.BlockSpec((1, 1, d), per_b),
        ],
        out_specs=[pl.BlockSpec((1, tm, d), tok), pl.BlockSpec((1, tm, d), tok)],
        out_shape=[jax.ShapeDtypeStruct((nb, length, d), F32),
                   jax.ShapeDtypeStruct((nb, length, d), BF16)],
        compiler_params=_params("arbitrary", "arbitrary"),
        name="out_proj",
    )(y_ssd, y_sc, w_out, x, gate, g.reshape(1, d), shift, scale)


def _ffn_up_kernel(h_ref, wg_ref, wu_ref, o_ref, wgb_ref, wub_ref, *, sub_m):
    @pl.when((pl.program_id(1) == 0) & (pl.program_id(2) == 0))
    def _():
        wgb_ref[...] = wg_ref[0].astype(BF16)
        wub_ref[...] = wu_ref[0].astype(BF16)

    tm = h_ref.shape[1]
    for m0 in range(0, tm, sub_m):
        h = h_ref[0, m0:m0 + sub_m, :]
        o_ref[0, m0:m0 + sub_m, :] = (_silu(_dot(h, wgb_ref[...])) * _dot(h, wub_ref[...])).astype(o_ref.dtype)


def ffn_up(h, w_gate, w_up, layer, tm=2048, tn=512, sub_m=1024):
    nb, length, d = h.shape
    n = w_gate.shape[-1]
    tm = min(tm, length)
    w_spec = pl.BlockSpec((1, d, tn), lambda j, b, i: (layer, 0, j))
    return pl.pallas_call(
        functools.partial(_ffn_up_kernel, sub_m=sub_m),
        grid=(n // tn, nb, length // tm),
        in_specs=[pl.BlockSpec((1, tm, d), lambda j, b, i: (b, i, 0)), w_spec, w_spec],
        out_specs=pl.BlockSpec((1, tm, tn), lambda j, b, i: (b, i, j)),
        out_shape=jax.ShapeDtypeStruct((nb, length, n), BF16),
        scratch_shapes=[pltpu.VMEM((d, tn), BF16), pltpu.VMEM((d, tn), BF16)],
        compiler_params=_params("arbitrary", "arbitrary", "arbitrary"),
        name="ffn_up",
    )(h, w_gate, w_up)


def _ffn_down_kernel(a_ref, w_ref, x_ref, gate_ref, g_ref, sh_ref, sc_ref, *refs, sub_n):
    res_ref, h_ref = refs[0], refs[-1]
    n = res_ref.shape[2]
    a = a_ref[0]
    for n0 in range(0, n, sub_n):
        ns = slice(n0, n0 + sub_n)
        res_ref[0, :, ns] = x_ref[0, :, ns] + gate_ref[0, :, ns] * _dot(a, w_ref[:, ns])
    hn = _norm_rows(res_ref[0], g_ref[...])
    h_ref[0] = (hn * (1.0 + sc_ref[0]) + sh_ref[0]).astype(h_ref.dtype)


def ffn_down(act, w_down, x, gate, g, shift, scale, emit_x, h_dtype, tm=512):
    nb, length, d = x.shape
    kk = act.shape[-1]
    tok = lambda b, i: (b, i, 0)
    per_b = lambda b, i: (b, 0, 0)
    if emit_x:
        out_shape = [jax.ShapeDtypeStruct((nb, length, d), F32), jax.ShapeDtypeStruct((nb, length, d), h_dtype)]
    else:
        assert h_dtype == F32
        out_shape = [jax.ShapeDtypeStruct((nb, length, d), F32)]
    return pl.pallas_call(
        functools.partial(_ffn_down_kernel, sub_n=d // 2),
        grid=(nb, length // tm),
        in_specs=[
            pl.BlockSpec((1, tm, kk), tok),
            pl.BlockSpec((kk, d), lambda b, i: (0, 0), pipeline_mode=pl.Buffered(1)),
            pl.BlockSpec((1, tm, d), tok),
            pl.BlockSpec((1, 1, d), per_b),
            pl.BlockSpec((1, d), lambda b, i: (0, 0)),
            pl.BlockSpec((1, 1, d), per_b),
            pl.BlockSpec((1, 1, d), per_b),
        ],
        out_specs=[pl.BlockSpec((1, tm, d), tok)] * len(out_shape),
        out_shape=out_shape,
        compiler_params=_params("arbitrary", "arbitrary", vmem=FFN_DOWN_VMEM_LIMIT),
        name="ffn_down",
    )(act, w_down, x, gate, g.reshape(1, d), shift, scale)


def _to_col_major(t, rows):
    b, length, ch = t.shape
    return t.reshape(b, rows, GRID_W, ch).transpose(0, 2, 1, 3).reshape(b, length, ch)


def _from_col_major(t, rows):
    b, length, ch = t.shape
    return t.reshape(b, GRID_W, rows, ch).transpose(0, 2, 1, 3).reshape(b, length, ch)


def _mixer(h_rows, nseq, w_main, n_main, w_dt, layer, h0_f, h0_b):
    (conv_w, conv_b, dt_bias, a_log, d_skip, ssd_g, sc_w) = layer
    nb, length, _ = h_rows.shape
    t = nb * length // nseq
    tm = min(length, 1024)
    p = matmul(h_rows, w_main, n_main, BF16, tm=tm, tn=n_main // 2, sub=256)
    dt_raw = matmul(h_rows, w_dt, LANES, F32, tm=tm, tn=LANES, sub=LANES)
    p = p.reshape(nseq, t, n_main)
    dt_raw = dt_raw.reshape(nseq, t, LANES)
    xbc = conv_xbc(p, conv_w, conv_b)
    y_ssd, h_f, h_b = ssd(xbc, dt_raw, p, h0_f, h0_b, dt_bias, a_log, d_skip, ssd_g)
    y_sc = conv_sc(p, sc_w) if n_main == MAIN_COLS else None
    return y_ssd, y_sc, h_f, h_b


def kernel(x, c, ctx, c_ctx, ada_w, ada_b, mix_norm_g, w_in, ssd_conv_w, ssd_conv_b, ssd_dt_bias,
           ssd_a_log, ssd_d, ssd_norm_g, sc_conv_w, w_out, ffn_norm_g, w_gate, w_up, w_down,
           final_norm_g):
    batch, seq, d = x.shape
    depth = ada_w.shape[0]
    ctx_len = ctx.shape[1]
    rows = seq // GRID_W
    n_ctx_rows = batch * ctx_len

    cc = jnp.zeros((16, d), F32).at[:batch].set(c).at[batch].set(c_ctx)
    mods = adaln(cc, ada_w, ada_b)

    def mod_vectors(l):
        mx = [mods[l, :batch, i * d:(i + 1) * d].reshape(batch, 1, d) for i in range(6)]
        mc = [mods[l, batch:batch + 1, i * d:(i + 1) * d].reshape(1, 1, d) for i in range(6)]
        return mx, mc

    dt_lo = SSD_WIDTH + SSD_XBC
    w_main = [jnp.concatenate([w_in[l][:, :dt_lo], w_in[l][:, SSD_COLS:]], axis=1).astype(BF16)
              for l in range(depth)]
    w_dt = [jnp.tile(w_in[l][:, dt_lo:SSD_COLS].astype(BF16), (1, LANES // HEAD_DIRS))
            for l in range(depth)]
    w_out_b = w_out.astype(BF16)
    w_down_b = w_down.astype(BF16)

    h_ctx = ctx.reshape(1, n_ctx_rows, d)
    mx, mc = mod_vectors(0)
    hx = norm_mod(x, mix_norm_g[0], mx[0], mx[1])
    hc = norm_mod(h_ctx, mix_norm_g[0], mc[0], mc[1])
    zero_state = jnp.zeros((batch, SSD_STATE, SSD_WIDTH), F32)
    zeros_d = jnp.zeros((batch, 1, d), F32)
    out = None

    for l in range(depth):
        last = l == depth - 1
        layer = (ssd_conv_w[l], ssd_conv_b[l], ssd_dt_bias[l], ssd_a_log[l], ssd_d[l],
                 ssd_norm_g[l], sc_conv_w[l])
        if not last:
            mx_next, mc_next = mod_vectors(l + 1)

        if last:
            _, _, state_f, state_b = _mixer(hc, batch, w_main[l], dt_lo, w_dt[l], layer,
                                            zero_state, zero_state)
        else:
            y_ssd, y_sc, state_f, state_b = _mixer(hc, batch, w_main[l], MAIN_COLS, w_dt[l], layer,
                                                   zero_state, zero_state)
            h_ctx, hf = out_proj(y_ssd.reshape(1, n_ctx_rows, -1), y_sc.reshape(1, n_ctx_rows, -1),
                                 w_out_b, l, h_ctx, mc[2], ffn_norm_g[l], mc[3], mc[4])
            act = ffn_up(hf, w_gate, w_up, l)
            h_ctx, hc = ffn_down(act, w_down_b[l], h_ctx, mc[5], mix_norm_g[l + 1],
                                 mc_next[0], mc_next[1], emit_x=True, h_dtype=BF16)

        col_major = l % 2 == 1
        if col_major:
            hx = _to_col_major(hx, rows)
        y_ssd, y_sc, _, _ = _mixer(hx, batch, w_main[l], MAIN_COLS, w_dt[l], layer, state_f, state_b)
        if col_major:
            y_ssd = _from_col_major(y_ssd, rows)
            y_sc = _from_col_major(y_sc, rows)
        x, hf = out_proj(y_ssd, y_sc, w_out_b, l, x, mx[2], ffn_norm_g[l], mx[3], mx[4])
        act = ffn_up(hf, w_gate, w_up, l)
        if last:
            (out,) = ffn_down(act, w_down_b[l], x, mx[5], final_norm_g, zeros_d, zeros_d,
                              emit_x=False, h_dtype=x.dtype)
        else:
            x, hx = ffn_down(act, w_down_b[l], x, mx[5], mix_norm_g[l + 1],
                             mx_next[0], mx_next[1], emit_x=True, h_dtype=BF16)
            mx, mc = mx_next, mc_next
    return out
```

```python
import functools

import jax
import jax.numpy as jnp
import numpy as np
from jax import lax
from jax.experimental import pallas as pl
from jax.experimental.pallas import tpu as pltpu

F32 = jnp.float32
BF16 = jnp.bfloat16

D_MODEL = 2048
GRID_W = 64
NORM_EPS = 1e-6
SSD_WIDTH = 1024
SSD_HEAD_DIM = 64
SSD_HEADS = 16
SSD_GROUPS = 2
SSD_STATE = 128
SSD_CONV_W = 5
SSD_CHUNK = 128
SSD_CHUNKS_PER_STEP = 2
SSD_GN = SSD_GROUPS * SSD_STATE
SSD_XBC = SSD_WIDTH + 2 * SSD_GN
SSD_COLS = SSD_WIDTH + SSD_XBC + 2 * SSD_HEADS
SC_WIDTH = 1024
SC_CONV_W = 3
FFN_HIDDEN = 5632
LANES = 128
HEAD_GROUP_COLS = SSD_WIDTH // SSD_GROUPS
VMEM_LIMIT = 56 * 1024 * 1024
FFN_DOWN_VMEM_LIMIT = 62 * 1024 * 1024
LOG2_E = 1.4426950408889634


def _params(*sem, vmem=VMEM_LIMIT):
    return pltpu.CompilerParams(dimension_semantics=sem, vmem_limit_bytes=vmem)


def _silu(v):
    return v * jax.nn.sigmoid(v)


def _softplus(v):
    return jnp.maximum(v, 0.0) + jnp.log1p(jnp.exp(-jnp.abs(v)))


def _norm_rows(v, g):
    ms = jnp.mean(v * v, axis=-1, keepdims=True)
    return v * lax.rsqrt(ms + NORM_EPS) * g


def _dot(a, b):
    return jnp.dot(a, b, preferred_element_type=F32)


def _adaln_kernel(c_ref, w_ref, b_ref, o_ref):
    s = _silu(c_ref[...]).astype(BF16)
    o_ref[0] = _dot(s, w_ref[0].astype(BF16)) + b_ref[0]


def adaln(cc, ada_w, ada_b, tn=1536):
    depth, d, n = ada_w.shape
    rows = cc.shape[0]
    return pl.pallas_call(
        _adaln_kernel,
        grid=(depth, n // tn),
        in_specs=[
            pl.BlockSpec((rows, d), lambda l, j: (0, 0)),
            pl.BlockSpec((1, d, tn), lambda l, j: (l, 0, j)),
            pl.BlockSpec((1, 1, tn), lambda l, j: (l, 0, j)),
        ],
        out_specs=pl.BlockSpec((1, rows, tn), lambda l, j: (l, 0, j)),
        out_shape=jax.ShapeDtypeStruct((depth, rows, n), F32),
        compiler_params=_params("arbitrary", "arbitrary"),
        name="adaln",
    )(cc, ada_w, ada_b.reshape(depth, 1, n))


def _norm_mod_kernel(x_ref, g_ref, sh_ref, sc_ref, o_ref):
    y = _norm_rows(x_ref[0], g_ref[...])
    o_ref[0] = (y * (1.0 + sc_ref[0]) + sh_ref[0]).astype(o_ref.dtype)


def norm_mod(x, g, shift, scale, tr=512):
    nb, length, d = x.shape
    return pl.pallas_call(
        _norm_mod_kernel,
        grid=(nb, length // tr),
        in_specs=[
            pl.BlockSpec((1, tr, d), lambda b, i: (b, i, 0)),
            pl.BlockSpec((1, d), lambda b, i: (0, 0)),
            pl.BlockSpec((1, 1, d), lambda b, i: (b, 0, 0)),
            pl.BlockSpec((1, 1, d), lambda b, i: (b, 0, 0)),
        ],
        out_specs=pl.BlockSpec((1, tr, d), lambda b, i: (b, i, 0)),
        out_shape=jax.ShapeDtypeStruct((nb, length, d), BF16),
        compiler_params=_params("arbitrary", "arbitrary"),
        name="norm_mod",
    )(x, g.reshape(1, d), shift, scale)


def _matmul_kernel(a_ref, w_ref, o_ref, *, sub):
    a = a_ref[0]
    tn = o_ref.shape[-1]
    for n0 in range(0, tn, sub):
        o_ref[0, :, n0:n0 + sub] = _dot(a, w_ref[:, n0:n0 + sub]).astype(o_ref.dtype)


def matmul(a, w, n, out_dtype, tm, tn, sub):
    nb, length, k = a.shape
    return pl.pallas_call(
        functools.partial(_matmul_kernel, sub=sub),
        grid=(n // tn, nb, length // tm),
        in_specs=[
            pl.BlockSpec((1, tm, k), lambda j, b, i: (b, i, 0)),
            pl.BlockSpec((k, tn), lambda j, b, i: (0, j)),
        ],
        out_specs=pl.BlockSpec((1, tm, tn), lambda j, b, i: (b, i, j)),
        out_shape=jax.ShapeDtypeStruct((nb, length, n), out_dtype),
        compiler_params=_params("arbitrary", "arbitrary", "arbitrary"),
        name="matmul",
    )(a, w)


def _in_proj_ssd_kernel(a_ref, w_ref, wdt_ref, o_ref, dt_ref, *, sub):
    a = a_ref[0]
    for n0 in range(0, o_ref.shape[-1], sub):
        o_ref[0, :, n0:n0 + sub] = _dot(a, w_ref[:, n0:n0 + sub]).astype(o_ref.dtype)
    dt_ref[0] = _dot(a, wdt_ref[...])


def in_proj_ssd(a, w, w_dt, tm, sub=256):
    nb, length, k = a.shape
    n = w.shape[1]
    tok = lambda b, i: (b, i, 0)
    const = lambda b, i: (0, 0)
    return pl.pallas_call(
        functools.partial(_in_proj_ssd_kernel, sub=sub),
        grid=(nb, length // tm),
        in_specs=[pl.BlockSpec((1, tm, k), tok), pl.BlockSpec((k, n), const), pl.BlockSpec((k, LANES), const)],
        out_specs=[pl.BlockSpec((1, tm, n), tok), pl.BlockSpec((1, tm, LANES), tok)],
        out_shape=[jax.ShapeDtypeStruct((nb, length, n), BF16), jax.ShapeDtypeStruct((nb, length, LANES), F32)],
        compiler_params=_params("arbitrary", "arbitrary"),
        name="in_proj_ssd",
    )(a, w, w_dt)


def _shifted_taps(pad_ref, r0, rows, taps):
    half = taps // 2
    win = pad_ref[r0:r0 + rows + 32, :]
    out = []
    for k in range(taps):
        shift = (half - k) % (rows + 32)
        rolled = win if shift == 0 else pltpu.roll(win, shift, axis=0)
        out.append(rolled[16:16 + rows, :])
    return out


def _fill_padded(pad_ref, vals):
    t = vals.shape[0]
    zeros = jnp.zeros((16, vals.shape[1]), F32)
    pad_ref[0:16, :] = zeros
    pad_ref[t + 16:t + 32, :] = zeros
    pad_ref[16:t + 16, :] = vals


def _conv_xbc_kernel(x_ref, w_ref, b_ref, o_ref, pad_ref, *, rows):
    t = x_ref.shape[1]
    _fill_padded(pad_ref, x_ref[0].astype(F32))
    for r0 in range(0, t, rows):
        taps = _shifted_taps(pad_ref, r0, rows, SSD_CONV_W)
        acc = b_ref[...] + w_ref[0:1, :] * taps[0]
        for k in range(1, SSD_CONV_W):
            acc = acc + w_ref[k:k + 1, :] * taps[k]
        o_ref[0, r0:r0 + rows, :] = _silu(acc).astype(o_ref.dtype)


def conv_xbc(p, conv_w, conv_b, ct=256):
    nseq, t, _ = p.shape
    off = SSD_WIDTH // ct
    w8 = jnp.zeros((8, SSD_XBC), F32).at[:SSD_CONV_W].set(conv_w)
    return pl.pallas_call(
        functools.partial(_conv_xbc_kernel, rows=min(t, 256)),
        grid=(nseq, SSD_XBC // ct),
        in_specs=[
            pl.BlockSpec((1, t, ct), lambda s, j: (s, 0, off + j)),
            pl.BlockSpec((8, ct), lambda s, j: (0, j)),
            pl.BlockSpec((1, ct), lambda s, j: (0, j)),
        ],
        out_specs=pl.BlockSpec((1, t, ct), lambda s, j: (s, 0, j)),
        out_shape=jax.ShapeDtypeStruct((nseq, t, SSD_XBC), BF16),
        scratch_shapes=[pltpu.VMEM((t + 32, ct), F32)],
        compiler_params=_params("arbitrary", "arbitrary"),
        name="conv_xbc",
    )(p, w8, conv_b.reshape(1, SSD_XBC))


def _conv_sc_kernel(gb_ref, gc_ref, v_ref, w_ref, o_ref, pad_ref, *, rows):
    t = v_ref.shape[1]
    _fill_padded(pad_ref, gc_ref[0].astype(F32) * v_ref[0].astype(F32))
    for r0 in range(0, t, rows):
        taps = _shifted_taps(pad_ref, r0, rows, SC_CONV_W)
        acc = w_ref[0:1, :] * taps[0]
        for k in range(1, SC_CONV_W):
            acc = acc + w_ref[k:k + 1, :] * taps[k]
        gate = gb_ref[0, r0:r0 + rows, :].astype(F32)
        o_ref[0, r0:r0 + rows, :] = (gate * acc).astype(o_ref.dtype)


def conv_sc(p, sc_w, ct=256):
    nseq, t, _ = p.shape
    base = 0
    step = SC_WIDTH // ct
    w8 = jnp.zeros((8, SC_WIDTH), F32).at[:SC_CONV_W].set(sc_w)
    col = lambda k: pl.BlockSpec((1, t, ct), lambda s, j: (s, 0, base + k * step + j))
    return pl.pallas_call(
        functools.partial(_conv_sc_kernel, rows=min(t, 256)),
        grid=(nseq, SC_WIDTH // ct),
        in_specs=[col(0), col(1), col(2), pl.BlockSpec((8, ct), lambda s, j: (0, j))],
        out_specs=pl.BlockSpec((1, t, ct), lambda s, j: (s, 0, j)),
        out_shape=jax.ShapeDtypeStruct((nseq, t, SC_WIDTH), BF16),
        scratch_shapes=[pltpu.VMEM((t + 32, ct), F32)],
        compiler_params=_params("arbitrary", "arbitrary"),
        name="conv_sc",
    )(p, p, p, w8)


def _split3(v):
    hi = v.astype(BF16)
    r1 = v - hi.astype(F32)
    mid = r1.astype(BF16)
    return hi, mid, (r1 - mid.astype(F32)).astype(BF16)


def _cumsums(v, tri):
    return _dot(tri, jnp.concatenate(_split3(v), axis=0))


HEAD_DIRS = 2 * SSD_HEADS
SEL_ROWS = 2 * LANES
SEL_SPREAD = HEAD_DIRS * LANES
SEL_FWD = SEL_SPREAD + SSD_WIDTH
SEL_COLS = SEL_FWD + 2 * SSD_WIDTH


def _selection_matrix():
    s = np.zeros((SEL_ROWS, SEL_COLS), np.float32)
    for band in range(3):
        base = band * 2 * HEAD_DIRS
        for k in range(HEAD_DIRS):
            s[base + k, k * LANES:(k + 1) * LANES] = 1.0
        for h in range(SSD_HEADS):
            ch = slice(h * SSD_HEAD_DIM, (h + 1) * SSD_HEAD_DIM)
            s[base + HEAD_DIRS + h, SEL_SPREAD:SEL_FWD][ch] = 1.0
            s[base + SSD_HEADS + h, SEL_FWD:SEL_FWD + SSD_WIDTH][ch] = 1.0
            s[base + HEAD_DIRS + SSD_HEADS + h, SEL_FWD + SSD_WIDTH:SEL_COLS][ch] = 1.0
    return jnp.asarray(s, BF16)


def _packed_operand(acs, dt):
    lane = lax.broadcasted_iota(jnp.int32, acs.shape, 1)
    packed = jnp.where((lane & (2 * HEAD_DIRS - 1)) < HEAD_DIRS, acs, dt)
    hi, mid, lo = _split3(packed)
    first = jnp.where(lane < 2 * HEAD_DIRS, hi.astype(F32), mid.astype(F32)).astype(BF16)
    return jnp.concatenate([first, lo], axis=1)


def _state_update(st_ref, b_mat, xw, decay_row):
    for g in range(SSD_GROUPS):
        cols = slice(g * HEAD_GROUP_COLS, (g + 1) * HEAD_GROUP_COLS)
        contrib = lax.dot_general(
            b_mat[:, g * SSD_STATE:(g + 1) * SSD_STATE], xw[:, cols],
            (((0,), (0,)), ((), ())), preferred_element_type=F32)
        st_ref[:, cols] = st_ref[:, cols] * decay_row[:, cols] + contrib


def _ssd_kernel(xbc_ref, dt_ref, z_ref, h0f_ref, h0b_ref, dtb_ref, alog_ref, dsk_ref, g_ref, tri_ref,
                sel_ref, y_ref, hfl_ref, hbl_ref, hbe_ref, stf_ref, stb_ref, *, ng, per_step):
    q = SSD_CHUNK
    j = pl.program_id(1)
    fwd_lanes = 0
    bwd_lanes = SSD_HEADS

    def chunk_inputs(rows):
        dt = _softplus(dt_ref[0, rows, :] + dtb_ref[...])
        d_a = dt * (-jnp.exp(alog_ref[...]) * LOG2_E)
        xs = xbc_ref[0, rows, 0:SSD_WIDTH].astype(F32)
        b_mat = xbc_ref[0, rows, SSD_WIDTH:SSD_WIDTH + SSD_GN]
        return dt, d_a, xs, b_mat

    @pl.when(j == 0)
    def _():
        stf_ref[...] = h0f_ref[0]
        stb_ref[...] = h0b_ref[0]

    def backward_chunk(rows, c):
        dt, d_a, xs, b_mat = chunk_inputs(rows)
        acs_b = _cumsums(d_a, tri_ref[q:2 * q, :])
        spread = _dot(_packed_operand(acs_b, dt), sel_ref[:, SEL_FWD:SEL_COLS])
        sum_b = spread[:, 0:SSD_WIDTH]
        tot = sum_b[0:1, :]
        w_full = spread[:, SSD_WIDTH:2 * SSD_WIDTH] * jnp.exp2(tot - sum_b)
        xw = (xs * w_full).astype(BF16)
        hbe_ref[c] = stb_ref[...].astype(BF16)
        _state_update(stb_ref, b_mat, xw, jnp.exp2(tot))

    @pl.when(j < ng)
    def _():
        for sub in reversed(range(per_step)):
            backward_chunk(slice(sub * q, (sub + 1) * q), (ng - 1 - j) * per_step + sub)

        @pl.when(j == ng - 1)
        def _():
            hbl_ref[0] = stb_ref[...]

    def forward_chunk(rows, c):
        dt, d_a, xs, b_mat = chunk_inputs(rows)
        row = lax.broadcasted_iota(jnp.int32, (q, q), 0)
        lane = lax.broadcasted_iota(jnp.int32, (q, q), 1)
        sums = _cumsums(d_a, tri_ref[...])
        acs = jnp.where((lane & (HEAD_DIRS - 1)) < bwd_lanes, sums[0:q], sums[q:2 * q])
        spread = _dot(_packed_operand(acs, dt), sel_ref[:, 0:SEL_FWD])
        acs_t = acs.T
        dt_t = dt.T
        src_t = acs_t - jnp.log2(dt_t)
        both_t = jnp.log2(dt_t[fwd_lanes:fwd_lanes + SSD_HEADS] + dt_t[bwd_lanes:bwd_lanes + SSD_HEADS])
        c_mat = xbc_ref[0, rows, SSD_WIDTH + SSD_GN:SSD_XBC]
        below = row > lane
        diag = row == lane

        cb = []
        for g in range(SSD_GROUPS):
            gs = slice(g * SSD_STATE, (g + 1) * SSD_STATE)
            cb.append(lax.dot_general(c_mat[:, gs], b_mat[:, gs], (((1,), (1,)), ((), ())),
                                      preferred_element_type=F32))

        y_parts = []
        for i in range(SSD_HEADS // 2):
            mats = []
            for h in (2 * i, 2 * i + 1):
                hf, hb = fwd_lanes + h, bwd_lanes + h
                tgt = jnp.where(below, spread[:, hf * LANES:(hf + 1) * LANES],
                                spread[:, hb * LANES:(hb + 1) * LANES])
                src = jnp.where(below, src_t[hf:hf + 1, :], src_t[hb:hb + 1, :])
                arg = jnp.where(diag, both_t[h:h + 1, :], tgt - src)
                mats.append((cb[h // (SSD_HEADS // SSD_GROUPS)] * jnp.exp2(arg)).astype(BF16))
            pair = xs[:, i * LANES:(i + 1) * LANES]
            top = jnp.where(lane < SSD_HEAD_DIM, pair, 0.0).astype(BF16)
            bot = jnp.where(lane >= SSD_HEAD_DIM, pair, 0.0).astype(BF16)
            y_parts.append(_dot(jnp.concatenate(mats, axis=1), jnp.concatenate([top, bot], axis=0)))
        y = jnp.concatenate(y_parts, axis=1)

        half = (lax.broadcasted_iota(jnp.int32, (q, SSD_WIDTH), 1) & (LANES - 1)) < SSD_HEAD_DIM

        def per_channel(first_block):
            even = [spread[:, (first_block + 2 * i) * LANES:(first_block + 2 * i + 1) * LANES]
                    for i in range(SSD_HEADS // 2)]
            odd = [spread[:, (first_block + 2 * i + 1) * LANES:(first_block + 2 * i + 2) * LANES]
                   for i in range(SSD_HEADS // 2)]
            return jnp.where(half, jnp.concatenate(even, axis=1), jnp.concatenate(odd, axis=1))

        sum_f = per_channel(fwd_lanes)
        e_f = jnp.exp2(sum_f)
        e_b = jnp.exp2(per_channel(bwd_lanes))
        st_f = stf_ref[...].astype(BF16)
        st_b = hbe_ref[c]
        off_f, off_b = [], []
        for g in range(SSD_GROUPS):
            gs = slice(g * SSD_STATE, (g + 1) * SSD_STATE)
            cols = slice(g * HEAD_GROUP_COLS, (g + 1) * HEAD_GROUP_COLS)
            off_f.append(_dot(c_mat[:, gs], st_f[:, cols]))
            off_b.append(_dot(c_mat[:, gs], st_b[:, cols]))
        y = y + e_f * jnp.concatenate(off_f, axis=1) + e_b * jnp.concatenate(off_b, axis=1)
        y = y + dsk_ref[...] * xs

        w_full = spread[:, SEL_SPREAD:SEL_FWD] * jnp.exp2(sum_f[q - 1:q, :] - sum_f)
        _state_update(stf_ref, b_mat, (xs * w_full).astype(BF16), e_f[q - 1:q, :])

        z = z_ref[0, rows, :].astype(F32)
        y_ref[0, rows, :] = _norm_rows(y * _silu(z), g_ref[...]).astype(y_ref.dtype)

    @pl.when(j >= ng)
    def _():
        for sub in range(per_step):
            forward_chunk(slice(sub * q, (sub + 1) * q), (j - ng) * per_step + sub)

        @pl.when(j == 2 * ng - 1)
        def _():
            hfl_ref[0] = stf_ref[...]


def ssd(xbc, dt_raw, p, h0_f, h0_b, dt_bias, a_log, d_skip, norm_g):
    nseq, t, _ = xbc.shape
    nc = t // SSD_CHUNK
    per_step = SSD_CHUNKS_PER_STEP
    ng = nc // per_step
    rows = per_step * SSD_CHUNK
    pad_row = lambda v: jnp.tile(v.reshape(1, HEAD_DIRS), (1, LANES // HEAD_DIRS))
    ones = np.ones((SSD_CHUNK, SSD_CHUNK), np.float32)
    tri = jnp.asarray(np.concatenate([np.tile(np.tril(ones), (1, 3)), np.tile(np.triu(ones), (1, 3))]), BF16)
    chunk = lambda s, j: (s, jnp.where(j < ng, ng - 1 - j, j - ng), 0)
    out_chunk = lambda s, j: (s, jnp.maximum(j - ng, 0), 0)
    const2 = lambda s, j: (0, 0)
    seq = lambda s, j: (s, 0, 0)
    state_shape = jax.ShapeDtypeStruct((nseq, SSD_STATE, SSD_WIDTH), F32)
    return pl.pallas_call(
        functools.partial(_ssd_kernel, ng=ng, per_step=per_step),
        grid=(nseq, 2 * ng),
        in_specs=[
            pl.BlockSpec((1, rows, SSD_XBC), chunk),
            pl.BlockSpec((1, rows, LANES), chunk),
            pl.BlockSpec((1, rows, SSD_WIDTH), out_chunk),
            pl.BlockSpec((1, SSD_STATE, SSD_WIDTH), seq),
            pl.BlockSpec((1, SSD_STATE, SSD_WIDTH), seq),
            pl.BlockSpec((1, LANES), const2),
            pl.BlockSpec((1, LANES), const2),
            pl.BlockSpec((1, SSD_WIDTH), const2),
            pl.BlockSpec((1, SSD_WIDTH), const2),
            pl.BlockSpec((2 * SSD_CHUNK, 3 * SSD_CHUNK), const2),
            pl.BlockSpec((SEL_ROWS, SEL_COLS), const2),
        ],
        out_specs=[
            pl.BlockSpec((1, rows, SSD_WIDTH), out_chunk),
            pl.BlockSpec((1, SSD_STATE, SSD_WIDTH), seq),
            pl.BlockSpec((1, SSD_STATE, SSD_WIDTH), seq),
        ],
        out_shape=[jax.ShapeDtypeStruct((nseq, t, SSD_WIDTH), BF16), state_shape, state_shape],
        scratch_shapes=[
            pltpu.VMEM((nc, SSD_STATE, SSD_WIDTH), BF16),
            pltpu.VMEM((SSD_STATE, SSD_WIDTH), F32),
            pltpu.VMEM((SSD_STATE, SSD_WIDTH), F32),
        ],
        compiler_params=_params("arbitrary", "arbitrary"),
        name="ssd",
    )(xbc, dt_raw, p, h0_f, h0_b, pad_row(dt_bias), pad_row(a_log),
      jnp.repeat(d_skip, SSD_HEAD_DIM).reshape(1, SSD_WIDTH), norm_g.reshape(1, SSD_WIDTH), tri,
      _selection_matrix())


def _out_proj_kernel(y1_ref, y2_ref, w_ref, x_ref, gate_ref, g_ref, sh_ref, sc_ref,
                     xo_ref, h_ref, *, sub):
    y = jnp.concatenate([y1_ref[0], y2_ref[0]], axis=1)
    n = xo_ref.shape[-1]
    for n0 in range(0, n, sub):
        ns = slice(n0, n0 + sub)
        xo_ref[0, :, ns] = x_ref[0, :, ns] + gate_ref[0, :, ns] * _dot(y, w_ref[0, :, ns])
    hn = _norm_rows(xo_ref[0], g_ref[...])
    h_ref[0] = (hn * (1.0 + sc_ref[0]) + sh_ref[0]).astype(h_ref.dtype)


def out_proj(y_ssd, y_sc, w_out, layer, x, gate, g, shift, scale, tm=512, sub=1024):
    nb, length, d = x.shape
    k1, k2 = y_ssd.shape[-1], y_sc.shape[-1]
    tok = lambda b, i: (b, i, 0)
    per_b = lambda b, i: (b, 0, 0)
    return pl.pallas_call(
        functools.partial(_out_proj_kernel, sub=sub),
        grid=(nb, length // tm),
        in_specs=[
            pl.BlockSpec((1, tm, k1), tok),
            pl.BlockSpec((1, tm, k2), tok),
            pl.BlockSpec((1, k1 + k2, d), lambda b, i: (layer, 0, 0)),
            pl.BlockSpec((1, tm, d), tok),
            pl.BlockSpec((1, 1, d), per_b),
            pl.BlockSpec((1, d), lambda b, i: (0, 0)),
            pl.BlockSpec((1, 1, d), per_b),
            pl.BlockSpec((1, 1, d), per_b),
        ],
        out_specs=[pl.BlockSpec((1, tm, d), tok), pl.BlockSpec((1, tm, d), tok)],
        out_shape=[jax.ShapeDtypeStruct((nb, length, d), F32),
                   jax.ShapeDtypeStruct((nb, length, d), BF16)],
        compiler_params=_params("arbitrary", "arbitrary"),
        name="out_proj",
    )(y_ssd, y_sc, w_out, x, gate, g.reshape(1, d), shift, scale)


def _ffn_up_kernel(h_ref, wg_ref, wu_ref, o_ref, wgb_ref, wub_ref, *, sub_m):
    @pl.when((pl.program_id(1) == 0) & (pl.program_id(2) == 0))
    def _():
        wgb_ref[...] = wg_ref[0].astype(BF16)
        wub_ref[...] = wu_ref[0].astype(BF16)

    tm = h_ref.shape[1]
    for m0 in range(0, tm, sub_m):
        h = h_ref[0, m0:m0 + sub_m, :]
        o_ref[0, m0:m0 + sub_m, :] = (_silu(_dot(h, wgb_ref[...])) * _dot(h, wub_ref[...])).astype(o_ref.dtype)


def ffn_up(h, w_gate, w_up, layer, tm=2048, tn=512, sub_m=1024):
    nb, length, d = h.shape
    n = w_gate.shape[-1]
    tm = min(tm, length)
    w_spec = pl.BlockSpec((1, d, tn), lambda j, b, i: (layer, 0, j))
    return pl.pallas_call(
        functools.partial(_ffn_up_kernel, sub_m=sub_m),
        grid=(n // tn, nb, length // tm),
        in_specs=[pl.BlockSpec((1, tm, d), lambda j, b, i: (b, i, 0)), w_spec, w_spec],
        out_specs=pl.BlockSpec((1, tm, tn), lambda j, b, i: (b, i, j)),
        out_shape=jax.ShapeDtypeStruct((nb, length, n), BF16),
        scratch_shapes=[pltpu.VMEM((d, tn), BF16), pltpu.VMEM((d, tn), BF16)],
        compiler_params=_params("arbitrary", "arbitrary", "arbitrary"),
        name="ffn_up",
    )(h, w_gate, w_up)


def _ffn_down_kernel(a_ref, w_ref, x_ref, gate_ref, g_ref, sh_ref, sc_ref, *refs, sub_n):
    res_ref, h_ref = refs[0], refs[-1]
    n = res_ref.shape[2]
    a = a_ref[0]
    for n0 in range(0, n, sub_n):
        ns = slice(n0, n0 + sub_n)
        res_ref[0, :, ns] = x_ref[0, :, ns] + gate_ref[0, :, ns] * _dot(a, w_ref[:, ns])
    hn = _norm_rows(res_ref[0], g_ref[...])
    h_ref[0] = (hn * (1.0 + sc_ref[0]) + sh_ref[0]).astype(h_ref.dtype)


def ffn_down(act, w_down, x, gate, g, shift, scale, emit_x, h_dtype, tm=512):
    nb, length, d = x.shape
    kk = act.shape[-1]
    tok = lambda b, i: (b, i, 0)
    per_b = lambda b, i: (b, 0, 0)
    if emit_x:
        out_shape = [jax.ShapeDtypeStruct((nb, length, d), F32), jax.ShapeDtypeStruct((nb, length, d), h_dtype)]
    else:
        assert h_dtype == F32
        out_shape = [jax.ShapeDtypeStruct((nb, length, d), F32)]
    return pl.pallas_call(
        functools.partial(_ffn_down_kernel, sub_n=d // 2),
        grid=(nb, length // tm),
        in_specs=[
            pl.BlockSpec((1, tm, kk), tok),
            pl.BlockSpec((kk, d), lambda b, i: (0, 0), pipeline_mode=pl.Buffered(1)),
            pl.BlockSpec((1, tm, d), tok),
            pl.BlockSpec((1, 1, d), per_b),
            pl.BlockSpec((1, d), lambda b, i: (0, 0)),
            pl.BlockSpec((1, 1, d), per_b),
            pl.BlockSpec((1, 1, d), per_b),
        ],
        out_specs=[pl.BlockSpec((1, tm, d), tok)] * len(out_shape),
        out_shape=out_shape,
        compiler_params=_params("arbitrary", "arbitrary", vmem=FFN_DOWN_VMEM_LIMIT),
        name="ffn_down",
    )(act, w_down, x, gate, g.reshape(1, d), shift, scale)


def _to_col_major(t, rows):
    b, length, ch = t.shape
    return t.reshape(b, rows, GRID_W, ch).transpose(0, 2, 1, 3).reshape(b, length, ch)


def _from_col_major(t, rows):
    b, length, ch = t.shape
    return t.reshape(b, GRID_W, rows, ch).transpose(0, 2, 1, 3).reshape(b, length, ch)


def _mixer(h_rows, nseq, w_ssd, w_dt, w_sc, layer, h0_f, h0_b):
    (conv_w, conv_b, dt_bias, a_log, d_skip, ssd_g, sc_w) = layer
    nb, length, _ = h_rows.shape
    t = nb * length // nseq
    tm = min(length, 1024)
    p, dt_raw = in_proj_ssd(h_rows, w_ssd, w_dt, tm=tm)
    p = p.reshape(nseq, t, p.shape[-1])
    dt_raw = dt_raw.reshape(nseq, t, LANES)
    xbc = conv_xbc(p, conv_w, conv_b)
    y_ssd, h_f, h_b = ssd(xbc, dt_raw, p, h0_f, h0_b, dt_bias, a_log, d_skip, ssd_g)
    y_sc = None
    if w_sc is not None:
        n_sc = w_sc.shape[1]
        p_sc = matmul(h_rows, w_sc, n_sc, BF16, tm=tm, tn=n_sc, sub=256)
        y_sc = conv_sc(p_sc.reshape(nseq, t, n_sc), sc_w)
    return y_ssd, y_sc, h_f, h_b


def kernel(x, c, ctx, c_ctx, ada_w, ada_b, mix_norm_g, w_in, ssd_conv_w, ssd_conv_b, ssd_dt_bias,
           ssd_a_log, ssd_d, ssd_norm_g, sc_conv_w, w_out, ffn_norm_g, w_gate, w_up, w_down,
           final_norm_g):
    batch, seq, d = x.shape
    depth = ada_w.shape[0]
    ctx_len = ctx.shape[1]
    rows = seq // GRID_W
    n_ctx_rows = batch * ctx_len

    cc = jnp.zeros((16, d), F32).at[:batch].set(c).at[batch].set(c_ctx)
    mods = adaln(cc, ada_w, ada_b)

    def mod_vectors(l):
        mx = [mods[l, :batch, i * d:(i + 1) * d].reshape(batch, 1, d) for i in range(6)]
        mc = [mods[l, batch:batch + 1, i * d:(i + 1) * d].reshape(1, 1, d) for i in range(6)]
        return mx, mc

    dt_lo = SSD_WIDTH + SSD_XBC
    w_ssd = [w_in[l][:, :dt_lo].astype(BF16) for l in range(depth)]
    w_sc = [w_in[l][:, SSD_COLS:].astype(BF16) for l in range(depth)]
    w_dt = [jnp.tile(w_in[l][:, dt_lo:SSD_COLS].astype(BF16), (1, LANES // HEAD_DIRS))
            for l in range(depth)]
    w_out_b = w_out.astype(BF16)
    w_down_b = w_down.astype(BF16)

    h_ctx = ctx.reshape(1, n_ctx_rows, d)
    mx, mc = mod_vectors(0)
    hx = norm_mod(x, mix_norm_g[0], mx[0], mx[1])
    hc = norm_mod(h_ctx, mix_norm_g[0], mc[0], mc[1])
    zero_state = jnp.zeros((batch, SSD_STATE, SSD_WIDTH), F32)
    zeros_d = jnp.zeros((batch, 1, d), F32)
    out = None

    for l in range(depth):
        last = l == depth - 1
        layer = (ssd_conv_w[l], ssd_conv_b[l], ssd_dt_bias[l], ssd_a_log[l], ssd_d[l],
                 ssd_norm_g[l], sc_conv_w[l])
        if not last:
            mx_next, mc_next = mod_vectors(l + 1)

        if last:
            _, _, state_f, state_b = _mixer(hc, batch, w_ssd[l], w_dt[l], None, layer,
                                            zero_state, zero_state)
        else:
            y_ssd, y_sc, state_f, state_b = _mixer(hc, batch, w_ssd[l], w_dt[l], w_sc[l], layer,
                                                   zero_state, zero_state)
            h_ctx, hf = out_proj(y_ssd.reshape(1, n_ctx_rows, -1), y_sc.reshape(1, n_ctx_rows, -1),
                                 w_out_b, l, h_ctx, mc[2], ffn_norm_g[l], mc[3], mc[4])
            act = ffn_up(hf, w_gate, w_up, l)
            h_ctx, hc = ffn_down(act, w_down_b[l], h_ctx, mc[5], mix_norm_g[l + 1],
                                 mc_next[0], mc_next[1], emit_x=True, h_dtype=BF16)

        col_major = l % 2 == 1
        if col_major:
            hx = _to_col_major(hx, rows)
        y_ssd, y_sc, _, _ = _mixer(hx, batch, w_ssd[l], w_dt[l], w_sc[l], layer, state_f, state_b)
        if col_major:
            y_ssd = _from_col_major(y_ssd, rows)
            y_sc = _from_col_major(y_sc, rows)
        x, hf = out_proj(y_ssd, y_sc, w_out_b, l, x, mx[2], ffn_norm_g[l], mx[3], mx[4])
        act = ffn_up(hf, w_gate, w_up, l)
        if last:
            (out,) = ffn_down(act, w_down_b[l], x, mx[5], final_norm_g, zeros_d, zeros_d,
                              emit_x=False, h_dtype=x.dtype)
        else:
            x, hx = ffn_down(act, w_down_b[l], x, mx[5], mix_norm_g[l + 1],
                             mx_next[0], mx_next[1], emit_x=True, h_dtype=BF16)
            mx, mc = mx_next, mc_next
    return out
```

```python
import functools

import jax
import jax.numpy as jnp
import numpy as np
from jax import lax
from jax.experimental import pallas as pl
from jax.experimental.pallas import tpu as pltpu

F32 = jnp.float32
BF16 = jnp.bfloat16

D_MODEL = 2048
GRID_W = 64
NORM_EPS = 1e-6
SSD_WIDTH = 1024
SSD_HEAD_DIM = 64
SSD_HEADS = 16
SSD_GROUPS = 2
SSD_STATE = 128
SSD_CONV_W = 5
SSD_CHUNK = 128
SSD_CHUNKS_PER_STEP = 4
SSD_GN = SSD_GROUPS * SSD_STATE
SSD_XBC = SSD_WIDTH + 2 * SSD_GN
SSD_COLS = SSD_WIDTH + SSD_XBC + 2 * SSD_HEADS
SC_WIDTH = 1024
SC_CONV_W = 3
FFN_HIDDEN = 5632
LANES = 128
HEAD_GROUP_COLS = SSD_WIDTH // SSD_GROUPS
VMEM_LIMIT = 56 * 1024 * 1024
FFN_DOWN_VMEM_LIMIT = 62 * 1024 * 1024
LOG2_E = 1.4426950408889634


def _params(*sem, vmem=VMEM_LIMIT):
    return pltpu.CompilerParams(dimension_semantics=sem, vmem_limit_bytes=vmem)


def _silu(v):
    return v * jax.nn.sigmoid(v)


def _softplus(v):
    return jnp.maximum(v, 0.0) + jnp.log1p(jnp.exp(-jnp.abs(v)))


def _norm_rows(v, g):
    ms = jnp.mean(v * v, axis=-1, keepdims=True)
    return v * lax.rsqrt(ms + NORM_EPS) * g


def _dot(a, b):
    return jnp.dot(a, b, preferred_element_type=F32)


def _adaln_kernel(c_ref, w_ref, b_ref, o_ref):
    s = _silu(c_ref[...]).astype(BF16)
    o_ref[0] = _dot(s, w_ref[0].astype(BF16)) + b_ref[0]


def adaln(cc, ada_w, ada_b, tn=1536):
    depth, d, n = ada_w.shape
    rows = cc.shape[0]
    return pl.pallas_call(
        _adaln_kernel,
        grid=(depth, n // tn),
        in_specs=[
            pl.BlockSpec((rows, d), lambda l, j: (0, 0)),
            pl.BlockSpec((1, d, tn), lambda l, j: (l, 0, j)),
            pl.BlockSpec((1, 1, tn), lambda l, j: (l, 0, j)),
        ],
        out_specs=pl.BlockSpec((1, rows, tn), lambda l, j: (l, 0, j)),
        out_shape=jax.ShapeDtypeStruct((depth, rows, n), F32),
        compiler_params=_params("arbitrary", "arbitrary"),
        name="adaln",
    )(cc, ada_w, ada_b.reshape(depth, 1, n))


def _norm_mod_kernel(x_ref, g_ref, sh_ref, sc_ref, o_ref):
    y = _norm_rows(x_ref[0], g_ref[...])
    o_ref[0] = (y * (1.0 + sc_ref[0]) + sh_ref[0]).astype(o_ref.dtype)


def norm_mod(x, g, shift, scale, tr=512):
    nb, length, d = x.shape
    return pl.pallas_call(
        _norm_mod_kernel,
        grid=(nb, length // tr),
        in_specs=[
            pl.BlockSpec((1, tr, d), lambda b, i: (b, i, 0)),
            pl.BlockSpec((1, d), lambda b, i: (0, 0)),
            pl.BlockSpec((1, 1, d), lambda b, i: (b, 0, 0)),
            pl.BlockSpec((1, 1, d), lambda b, i: (b, 0, 0)),
        ],
        out_specs=pl.BlockSpec((1, tr, d), lambda b, i: (b, i, 0)),
        out_shape=jax.ShapeDtypeStruct((nb, length, d), BF16),
        compiler_params=_params("arbitrary", "arbitrary"),
        name="norm_mod",
    )(x, g.reshape(1, d), shift, scale)


def _matmul_kernel(a_ref, w_ref, o_ref, *, sub):
    a = a_ref[0]
    tn = o_ref.shape[-1]
    for n0 in range(0, tn, sub):
        o_ref[0, :, n0:n0 + sub] = _dot(a, w_ref[0, :, n0:n0 + sub]).astype(o_ref.dtype)


def matmul(a, w, layer, out_dtype, tm, tn, sub):
    nb, length, k = a.shape
    n = w.shape[-1]
    return pl.pallas_call(
        functools.partial(_matmul_kernel, sub=sub),
        grid=(n // tn, nb, length // tm),
        in_specs=[
            pl.BlockSpec((1, tm, k), lambda j, b, i: (b, i, 0)),
            pl.BlockSpec((1, k, tn), lambda j, b, i: (layer, 0, j)),
        ],
        out_specs=pl.BlockSpec((1, tm, tn), lambda j, b, i: (b, i, j)),
        out_shape=jax.ShapeDtypeStruct((nb, length, n), out_dtype),
        compiler_params=_params("arbitrary", "arbitrary", "arbitrary"),
        name="matmul",
    )(a, w)


def _in_proj_ssd_kernel(a_ref, w_ref, wdt_ref, o_ref, dt_ref, *, sub):
    a = a_ref[0]
    for n0 in range(0, o_ref.shape[-1], sub):
        o_ref[0, :, n0:n0 + sub] = _dot(a, w_ref[0, :, n0:n0 + sub]).astype(o_ref.dtype)
    dt_ref[0] = _dot(a, wdt_ref[0])


def in_proj_ssd(a, w, w_dt, layer, tm, sub=256):
    nb, length, k = a.shape
    n = w.shape[-1]
    tok = lambda b, i: (b, i, 0)
    of_layer = lambda b, i: (layer, 0, 0)
    return pl.pallas_call(
        functools.partial(_in_proj_ssd_kernel, sub=sub),
        grid=(nb, length // tm),
        in_specs=[pl.BlockSpec((1, tm, k), tok), pl.BlockSpec((1, k, n), of_layer),
                  pl.BlockSpec((1, k, LANES), of_layer)],
        out_specs=[pl.BlockSpec((1, tm, n), tok), pl.BlockSpec((1, tm, LANES), tok)],
        out_shape=[jax.ShapeDtypeStruct((nb, length, n), BF16), jax.ShapeDtypeStruct((nb, length, LANES), F32)],
        compiler_params=_params("arbitrary", "arbitrary"),
        name="in_proj_ssd",
    )(a, w, w_dt)


def _shifted_taps(pad_ref, r0, rows, taps):
    half = taps // 2
    win = pad_ref[r0:r0 + rows + 32, :]
    out = []
    for k in range(taps):
        shift = (half - k) % (rows + 32)
        rolled = win if shift == 0 else pltpu.roll(win, shift, axis=0)
        out.append(rolled[16:16 + rows, :])
    return out


def _fill_padded(pad_ref, vals):
    t = vals.shape[0]
    zeros = jnp.zeros((16, vals.shape[1]), F32)
    pad_ref[0:16, :] = zeros
    pad_ref[t + 16:t + 32, :] = zeros
    pad_ref[16:t + 16, :] = vals


def _conv_xbc_kernel(x_ref, w_ref, b_ref, o_ref, pad_ref, *, rows):
    t = x_ref.shape[1]
    _fill_padded(pad_ref, x_ref[0].astype(F32))
    for r0 in range(0, t, rows):
        taps = _shifted_taps(pad_ref, r0, rows, SSD_CONV_W)
        acc = b_ref[...] + w_ref[0:1, :] * taps[0]
        for k in range(1, SSD_CONV_W):
            acc = acc + w_ref[k:k + 1, :] * taps[k]
        o_ref[0, r0:r0 + rows, :] = _silu(acc).astype(o_ref.dtype)


def conv_xbc(p, conv_w, conv_b, ct=256):
    nseq, t, _ = p.shape
    off = SSD_WIDTH // ct
    w8 = jnp.zeros((8, SSD_XBC), F32).at[:SSD_CONV_W].set(conv_w)
    return pl.pallas_call(
        functools.partial(_conv_xbc_kernel, rows=min(t, 256)),
        grid=(nseq, SSD_XBC // ct),
        in_specs=[
            pl.BlockSpec((1, t, ct), lambda s, j: (s, 0, off + j)),
            pl.BlockSpec((8, ct), lambda s, j: (0, j)),
            pl.BlockSpec((1, ct), lambda s, j: (0, j)),
        ],
        out_specs=pl.BlockSpec((1, t, ct), lambda s, j: (s, 0, j)),
        out_shape=jax.ShapeDtypeStruct((nseq, t, SSD_XBC), BF16),
        scratch_shapes=[pltpu.VMEM((t + 32, ct), F32)],
        compiler_params=_params("arbitrary", "arbitrary"),
        name="conv_xbc",
    )(p, w8, conv_b.reshape(1, SSD_XBC))


def _conv_sc_kernel(gb_ref, gc_ref, v_ref, w_ref, o_ref, pad_ref, *, rows):
    t = v_ref.shape[1]
    _fill_padded(pad_ref, gc_ref[0].astype(F32) * v_ref[0].astype(F32))
    for r0 in range(0, t, rows):
        taps = _shifted_taps(pad_ref, r0, rows, SC_CONV_W)
        acc = w_ref[0:1, :] * taps[0]
        for k in range(1, SC_CONV_W):
            acc = acc + w_ref[k:k + 1, :] * taps[k]
        gate = gb_ref[0, r0:r0 + rows, :].astype(F32)
        o_ref[0, r0:r0 + rows, :] = (gate * acc).astype(o_ref.dtype)


def conv_sc(p, sc_w, ct=256):
    nseq, t, _ = p.shape
    base = 0
    step = SC_WIDTH // ct
    w8 = jnp.zeros((8, SC_WIDTH), F32).at[:SC_CONV_W].set(sc_w)
    col = lambda k: pl.BlockSpec((1, t, ct), lambda s, j: (s, 0, base + k * step + j))
    return pl.pallas_call(
        functools.partial(_conv_sc_kernel, rows=min(t, 256)),
        grid=(nseq, SC_WIDTH // ct),
        in_specs=[col(0), col(1), col(2), pl.BlockSpec((8, ct), lambda s, j: (0, j))],
        out_specs=pl.BlockSpec((1, t, ct), lambda s, j: (s, 0, j)),
        out_shape=jax.ShapeDtypeStruct((nseq, t, SC_WIDTH), BF16),
        scratch_shapes=[pltpu.VMEM((t + 32, ct), F32)],
        compiler_params=_params("arbitrary", "arbitrary"),
        name="conv_sc",
    )(p, p, p, w8)


def _split3(v):
    hi = v.astype(BF16)
    r1 = v - hi.astype(F32)
    mid = r1.astype(BF16)
    return hi, mid, (r1 - mid.astype(F32)).astype(BF16)


def _cumsums(v, tri):
    return _dot(tri, jnp.concatenate(_split3(v), axis=0))


HEAD_DIRS = 2 * SSD_HEADS
SEL_ROWS = 2 * LANES
SEL_SPREAD = HEAD_DIRS * LANES
SEL_FWD = SEL_SPREAD + SSD_WIDTH
SEL_COLS = SEL_FWD + 2 * SSD_WIDTH


def _selection_matrix():
    s = np.zeros((SEL_ROWS, SEL_COLS), np.float32)
    for band in range(3):
        base = band * 2 * HEAD_DIRS
        for k in range(HEAD_DIRS):
            s[base + k, k * LANES:(k + 1) * LANES] = 1.0
        for h in range(SSD_HEADS):
            ch = slice(h * SSD_HEAD_DIM, (h + 1) * SSD_HEAD_DIM)
            s[base + HEAD_DIRS + h, SEL_SPREAD:SEL_FWD][ch] = 1.0
            s[base + SSD_HEADS + h, SEL_FWD:SEL_FWD + SSD_WIDTH][ch] = 1.0
            s[base + HEAD_DIRS + SSD_HEADS + h, SEL_FWD + SSD_WIDTH:SEL_COLS][ch] = 1.0
    return jnp.asarray(s, BF16)


def _packed_operand(acs, dt):
    lane = lax.broadcasted_iota(jnp.int32, acs.shape, 1)
    packed = jnp.where((lane & (2 * HEAD_DIRS - 1)) < HEAD_DIRS, acs, dt)
    hi, mid, lo = _split3(packed)
    first = jnp.where(lane < 2 * HEAD_DIRS, hi.astype(F32), mid.astype(F32)).astype(BF16)
    return jnp.concatenate([first, lo], axis=1)


def _state_update(st_ref, b_mat, xw, decay_row):
    for g in range(SSD_GROUPS):
        cols = slice(g * HEAD_GROUP_COLS, (g + 1) * HEAD_GROUP_COLS)
        contrib = lax.dot_general(
            b_mat[:, g * SSD_STATE:(g + 1) * SSD_STATE], xw[:, cols],
            (((0,), (0,)), ((), ())), preferred_element_type=F32)
        st_ref[:, cols] = st_ref[:, cols] * decay_row[:, cols] + contrib


def _ssd_kernel(xbc_ref, dt_ref, z_ref, h0f_ref, h0b_ref, dtb_ref, alog_ref, dsk_ref, g_ref, tri_ref,
                sel_ref, y_ref, hfl_ref, hbl_ref, hbe_ref, stf_ref, stb_ref, *, ng, per_step):
    q = SSD_CHUNK
    j = pl.program_id(1)
    fwd_lanes = 0
    bwd_lanes = SSD_HEADS

    def chunk_inputs(rows):
        dt = _softplus(dt_ref[0, rows, :] + dtb_ref[...])
        d_a = dt * (-jnp.exp(alog_ref[...]) * LOG2_E)
        xs = xbc_ref[0, rows, 0:SSD_WIDTH].astype(F32)
        b_mat = xbc_ref[0, rows, SSD_WIDTH:SSD_WIDTH + SSD_GN]
        return dt, d_a, xs, b_mat

    @pl.when(j == 0)
    def _():
        stf_ref[...] = h0f_ref[0]
        stb_ref[...] = h0b_ref[0]

    def backward_chunk(rows, c):
        dt, d_a, xs, b_mat = chunk_inputs(rows)
        acs_b = _cumsums(d_a, tri_ref[q:2 * q, :])
        spread = _dot(_packed_operand(acs_b, dt), sel_ref[:, SEL_FWD:SEL_COLS])
        sum_b = spread[:, 0:SSD_WIDTH]
        tot = sum_b[0:1, :]
        w_full = spread[:, SSD_WIDTH:2 * SSD_WIDTH] * jnp.exp2(tot - sum_b)
        xw = (xs * w_full).astype(BF16)
        hbe_ref[c] = stb_ref[...].astype(BF16)
        _state_update(stb_ref, b_mat, xw, jnp.exp2(tot))

    @pl.when(j < ng)
    def _():
        for sub in reversed(range(per_step)):
            backward_chunk(slice(sub * q, (sub + 1) * q), (ng - 1 - j) * per_step + sub)

        @pl.when(j == ng - 1)
        def _():
            hbl_ref[0] = stb_ref[...]

    def forward_chunk(rows, c):
        dt, d_a, xs, b_mat = chunk_inputs(rows)
        row = lax.broadcasted_iota(jnp.int32, (q, q), 0)
        lane = lax.broadcasted_iota(jnp.int32, (q, q), 1)
        sums = _cumsums(d_a, tri_ref[...])
        acs = jnp.where((lane & (HEAD_DIRS - 1)) < bwd_lanes, sums[0:q], sums[q:2 * q])
        spread = _dot(_packed_operand(acs, dt), sel_ref[:, 0:SEL_FWD])
        acs_t = acs.T
        dt_t = dt.T
        src_t = acs_t - jnp.log2(dt_t)
        both_t = jnp.log2(dt_t[fwd_lanes:fwd_lanes + SSD_HEADS] + dt_t[bwd_lanes:bwd_lanes + SSD_HEADS])
        c_mat = xbc_ref[0, rows, SSD_WIDTH + SSD_GN:SSD_XBC]
        below = row > lane
        diag = row == lane

        cb = []
        for g in range(SSD_GROUPS):
            gs = slice(g * SSD_STATE, (g + 1) * SSD_STATE)
            cb.append(lax.dot_general(c_mat[:, gs], b_mat[:, gs], (((1,), (1,)), ((), ())),
                                      preferred_element_type=F32))

        y_parts = []
        for i in range(SSD_HEADS // 2):
            mats = []
            for h in (2 * i, 2 * i + 1):
                hf, hb = fwd_lanes + h, bwd_lanes + h
                tgt = jnp.where(below, spread[:, hf * LANES:(hf + 1) * LANES],
                                spread[:, hb * LANES:(hb + 1) * LANES])
                src = jnp.where(below, src_t[hf:hf + 1, :], src_t[hb:hb + 1, :])
                arg = jnp.where(diag, both_t[h:h + 1, :], tgt - src)
                mats.append((cb[h // (SSD_HEADS // SSD_GROUPS)] * jnp.exp2(arg)).astype(BF16))
            pair = xs[:, i * LANES:(i + 1) * LANES]
            top = jnp.where(lane < SSD_HEAD_DIM, pair, 0.0).astype(BF16)
            bot = jnp.where(lane >= SSD_HEAD_DIM, pair, 0.0).astype(BF16)
            y_parts.append(_dot(jnp.concatenate(mats, axis=1), jnp.concatenate([top, bot], axis=0)))
        y = jnp.concatenate(y_parts, axis=1)

        half = (lax.broadcasted_iota(jnp.int32, (q, SSD_WIDTH), 1) & (LANES - 1)) < SSD_HEAD_DIM

        def per_channel(first_block):
            even = [spread[:, (first_block + 2 * i) * LANES:(first_block + 2 * i + 1) * LANES]
                    for i in range(SSD_HEADS // 2)]
            odd = [spread[:, (first_block + 2 * i + 1) * LANES:(first_block + 2 * i + 2) * LANES]
                   for i in range(SSD_HEADS // 2)]
            return jnp.where(half, jnp.concatenate(even, axis=1), jnp.concatenate(odd, axis=1))

        sum_f = per_channel(fwd_lanes)
        e_f = jnp.exp2(sum_f)
        e_b = jnp.exp2(per_channel(bwd_lanes))
        st_f = stf_ref[...].astype(BF16)
        st_b = hbe_ref[c]
        off_f, off_b = [], []
        for g in range(SSD_GROUPS):
            gs = slice(g * SSD_STATE, (g + 1) * SSD_STATE)
            cols = slice(g * HEAD_GROUP_COLS, (g + 1) * HEAD_GROUP_COLS)
            off_f.append(_dot(c_mat[:, gs], st_f[:, cols]))
            off_b.append(_dot(c_mat[:, gs], st_b[:, cols]))
        y = y + e_f * jnp.concatenate(off_f, axis=1) + e_b * jnp.concatenate(off_b, axis=1)
        y = y + dsk_ref[...] * xs

        w_full = spread[:, SEL_SPREAD:SEL_FWD] * jnp.exp2(sum_f[q - 1:q, :] - sum_f)
        _state_update(stf_ref, b_mat, (xs * w_full).astype(BF16), e_f[q - 1:q, :])

        z = z_ref[0, rows, :].astype(F32)
        y_ref[0, rows, :] = _norm_rows(y * _silu(z), g_ref[...]).astype(y_ref.dtype)

    @pl.when(j >= ng)
    def _():
        for sub in range(per_step):
            forward_chunk(slice(sub * q, (sub + 1) * q), (j - ng) * per_step + sub)

        @pl.when(j == 2 * ng - 1)
        def _():
            hfl_ref[0] = stf_ref[...]


def ssd(xbc, dt_raw, p, h0_f, h0_b, dt_bias, a_log, d_skip, norm_g):
    nseq, t, _ = xbc.shape
    nc = t // SSD_CHUNK
    per_step = min(SSD_CHUNKS_PER_STEP, nc)
    ng = nc // per_step
    rows = per_step * SSD_CHUNK
    pad_row = lambda v: jnp.tile(v.reshape(1, HEAD_DIRS), (1, LANES // HEAD_DIRS))
    ones = np.ones((SSD_CHUNK, SSD_CHUNK), np.float32)
    tri = jnp.asarray(np.concatenate([np.tile(np.tril(ones), (1, 3)), np.tile(np.triu(ones), (1, 3))]), BF16)
    chunk = lambda s, j: (s, jnp.where(j < ng, ng - 1 - j, j - ng), 0)
    out_chunk = lambda s, j: (s, jnp.maximum(j - ng, 0), 0)
    const2 = lambda s, j: (0, 0)
    seq = lambda s, j: (s, 0, 0)
    state_shape = jax.ShapeDtypeStruct((nseq, SSD_STATE, SSD_WIDTH), F32)
    return pl.pallas_call(
        functools.partial(_ssd_kernel, ng=ng, per_step=per_step),
        grid=(nseq, 2 * ng),
        in_specs=[
            pl.BlockSpec((1, rows, SSD_XBC), chunk),
            pl.BlockSpec((1, rows, LANES), chunk),
            pl.BlockSpec((1, rows, SSD_WIDTH), out_chunk),
            pl.BlockSpec((1, SSD_STATE, SSD_WIDTH), seq),
            pl.BlockSpec((1, SSD_STATE, SSD_WIDTH), seq),
            pl.BlockSpec((1, LANES), const2),
            pl.BlockSpec((1, LANES), const2),
            pl.BlockSpec((1, SSD_WIDTH), const2),
            pl.BlockSpec((1, SSD_WIDTH), const2),
            pl.BlockSpec((2 * SSD_CHUNK, 3 * SSD_CHUNK), const2),
            pl.BlockSpec((SEL_ROWS, SEL_COLS), const2),
        ],
        out_specs=[
            pl.BlockSpec((1, rows, SSD_WIDTH), out_chunk),
            pl.BlockSpec((1, SSD_STATE, SSD_WIDTH), seq),
            pl.BlockSpec((1, SSD_STATE, SSD_WIDTH), seq),
        ],
        out_shape=[jax.ShapeDtypeStruct((nseq, t, SSD_WIDTH), BF16), state_shape, state_shape],
        scratch_shapes=[
            pltpu.VMEM((nc, SSD_STATE, SSD_WIDTH), BF16),
            pltpu.VMEM((SSD_STATE, SSD_WIDTH), F32),
            pltpu.VMEM((SSD_STATE, SSD_WIDTH), F32),
        ],
        compiler_params=_params("arbitrary", "arbitrary"),
        name="ssd",
    )(xbc, dt_raw, p, h0_f, h0_b, pad_row(dt_bias), pad_row(a_log),
      jnp.repeat(d_skip, SSD_HEAD_DIM).reshape(1, SSD_WIDTH), norm_g.reshape(1, SSD_WIDTH), tri,
      _selection_matrix())


def _out_proj_kernel(y1_ref, y2_ref, w_ref, x_ref, gate_ref, g_ref, sh_ref, sc_ref,
                     xo_ref, h_ref, *, sub):
    y = jnp.concatenate([y1_ref[0], y2_ref[0]], axis=1)
    n = xo_ref.shape[-1]
    for n0 in range(0, n, sub):
        ns = slice(n0, n0 + sub)
        xo_ref[0, :, ns] = x_ref[0, :, ns] + gate_ref[0, :, ns] * _dot(y, w_ref[0, :, ns])
    hn = _norm_rows(xo_ref[0], g_ref[...])
    h_ref[0] = (hn * (1.0 + sc_ref[0]) + sh_ref[0]).astype(h_ref.dtype)


def out_proj(y_ssd, y_sc, w_out, layer, x, gate, g, shift, scale, tm=512, sub=1024):
    nb, length, d = x.shape
    k1, k2 = y_ssd.shape[-1], y_sc.shape[-1]
    tok = lambda b, i: (b, i, 0)
    per_b = lambda b, i: (b, 0, 0)
    return pl.pallas_call(
        functools.partial(_out_proj_kernel, sub=sub),
        grid=(nb, length // tm),
        in_specs=[
            pl.BlockSpec((1, tm, k1), tok),
            pl.BlockSpec((1, tm, k2), tok),
            pl.BlockSpec((1, k1 + k2, d), lambda b, i: (layer, 0, 0)),
            pl.BlockSpec((1, tm, d), tok),
            pl.BlockSpec((1, 1, d), per_b),
            pl.BlockSpec((1, d), lambda b, i: (0, 0)),
            pl.BlockSpec((1, 1, d), per_b),
            pl.BlockSpec((1, 1, d), per_b),
        ],
        out_specs=[pl.BlockSpec((1, tm, d), tok), pl.BlockSpec((1, tm, d), tok)],
        out_shape=[jax.ShapeDtypeStruct((nb, length, d), F32),
                   jax.ShapeDtypeStruct((nb, length, d), BF16)],
        compiler_params=_params("arbitrary", "arbitrary"),
        name="out_proj",
    )(y_ssd, y_sc, w_out, x, gate, g.reshape(1, d), shift, scale)


def _ffn_up_kernel(h_ref, wg_ref, wu_ref, o_ref, wgb_ref, wub_ref, *, sub_m):
    @pl.when((pl.program_id(1) == 0) & (pl.program_id(2) == 0))
    def _():
        wgb_ref[...] = wg_ref[0].astype(BF16)
        wub_ref[...] = wu_ref[0].astype(BF16)

    tm = h_ref.shape[1]
    for m0 in range(0, tm, sub_m):
        h = h_ref[0, m0:m0 + sub_m, :]
        o_ref[0, m0:m0 + sub_m, :] = (_silu(_dot(h, wgb_ref[...])) * _dot(h, wub_ref[...])).astype(o_ref.dtype)


def ffn_up(h, w_gate, w_up, layer, tm=2048, tn=512, sub_m=1024):
    nb, length, d = h.shape
    n = w_gate.shape[-1]
    tm = min(tm, length)
    w_spec = pl.BlockSpec((1, d, tn), lambda j, b, i: (layer, 0, j))
    return pl.pallas_call(
        functools.partial(_ffn_up_kernel, sub_m=sub_m),
        grid=(n // tn, nb, length // tm),
        in_specs=[pl.BlockSpec((1, tm, d), lambda j, b, i: (b, i, 0)), w_spec, w_spec],
        out_specs=pl.BlockSpec((1, tm, tn), lambda j, b, i: (b, i, j)),
        out_shape=jax.ShapeDtypeStruct((nb, length, n), BF16),
        scratch_shapes=[pltpu.VMEM((d, tn), BF16), pltpu.VMEM((d, tn), BF16)],
        compiler_params=_params("arbitrary", "arbitrary", "arbitrary"),
        name="ffn_up",
    )(h, w_gate, w_up)


def _ffn_down_kernel(a_ref, w_ref, x_ref, gate_ref, g_ref, sh_ref, sc_ref, *refs, sub_n):
    res_ref, h_ref = refs[0], refs[-1]
    n = res_ref.shape[2]
    a = a_ref[0]
    for n0 in range(0, n, sub_n):
        ns = slice(n0, n0 + sub_n)
        res_ref[0, :, ns] = x_ref[0, :, ns] + gate_ref[0, :, ns] * _dot(a, w_ref[0, :, ns])
    hn = _norm_rows(res_ref[0], g_ref[...])
    h_ref[0] = (hn * (1.0 + sc_ref[0]) + sh_ref[0]).astype(h_ref.dtype)


def ffn_down(act, w_down, layer, x, gate, g, shift, scale, emit_x, h_dtype, tm=512):
    nb, length, d = x.shape
    kk = act.shape[-1]
    tok = lambda b, i: (b, i, 0)
    per_b = lambda b, i: (b, 0, 0)
    if emit_x:
        out_shape = [jax.ShapeDtypeStruct((nb, length, d), F32), jax.ShapeDtypeStruct((nb, length, d), h_dtype)]
    else:
        assert h_dtype == F32
        out_shape = [jax.ShapeDtypeStruct((nb, length, d), F32)]
    return pl.pallas_call(
        functools.partial(_ffn_down_kernel, sub_n=d // 2),
        grid=(nb, length // tm),
        in_specs=[
            pl.BlockSpec((1, tm, kk), tok),
            pl.BlockSpec((1, kk, d), lambda b, i: (layer, 0, 0), pipeline_mode=pl.Buffered(1)),
            pl.BlockSpec((1, tm, d), tok),
            pl.BlockSpec((1, 1, d), per_b),
            pl.BlockSpec((1, d), lambda b, i: (0, 0)),
            pl.BlockSpec((1, 1, d), per_b),
            pl.BlockSpec((1, 1, d), per_b),
        ],
        out_specs=[pl.BlockSpec((1, tm, d), tok)] * len(out_shape),
        out_shape=out_shape,
        compiler_params=_params("arbitrary", "arbitrary", vmem=FFN_DOWN_VMEM_LIMIT),
        name="ffn_down",
    )(act, w_down, x, gate, g.reshape(1, d), shift, scale)


def _to_col_major(t, rows):
    b, length, ch = t.shape
    return t.reshape(b, rows, GRID_W, ch).transpose(0, 2, 1, 3).reshape(b, length, ch)


def _from_col_major(t, rows):
    b, length, ch = t.shape
    return t.reshape(b, GRID_W, rows, ch).transpose(0, 2, 1, 3).reshape(b, length, ch)


def _mixer(h_rows, nseq, w_ssd, w_dt, w_sc, l, layer, h0_f, h0_b):
    (conv_w, conv_b, dt_bias, a_log, d_skip, ssd_g, sc_w) = layer
    nb, length, _ = h_rows.shape
    t = nb * length // nseq
    tm = min(length, 1024)
    p, dt_raw = in_proj_ssd(h_rows, w_ssd, w_dt, l, tm=tm)
    p = p.reshape(nseq, t, p.shape[-1])
    dt_raw = dt_raw.reshape(nseq, t, LANES)
    xbc = conv_xbc(p, conv_w, conv_b)
    y_ssd, h_f, h_b = ssd(xbc, dt_raw, p, h0_f, h0_b, dt_bias, a_log, d_skip, ssd_g)
    y_sc = None
    if w_sc is not None:
        n_sc = w_sc.shape[-1]
        p_sc = matmul(h_rows, w_sc, l, BF16, tm=tm, tn=n_sc, sub=256)
        y_sc = conv_sc(p_sc.reshape(nseq, t, n_sc), sc_w)
    return y_ssd, y_sc, h_f, h_b


def kernel(x, c, ctx, c_ctx, ada_w, ada_b, mix_norm_g, w_in, ssd_conv_w, ssd_conv_b, ssd_dt_bias,
           ssd_a_log, ssd_d, ssd_norm_g, sc_conv_w, w_out, ffn_norm_g, w_gate, w_up, w_down,
           final_norm_g):
    batch, seq, d = x.shape
    depth = ada_w.shape[0]
    ctx_len = ctx.shape[1]
    rows = seq // GRID_W
    n_ctx_rows = batch * ctx_len

    cc = jnp.zeros((16, d), F32).at[:batch].set(c).at[batch].set(c_ctx)
    mods = adaln(cc, ada_w, ada_b)

    def mod_vectors(l):
        mx = [mods[l, :batch, i * d:(i + 1) * d].reshape(batch, 1, d) for i in range(6)]
        mc = [mods[l, batch:batch + 1, i * d:(i + 1) * d].reshape(1, 1, d) for i in range(6)]
        return mx, mc

    dt_lo = SSD_WIDTH + SSD_XBC
    w_ssd = w_in[:, :, :dt_lo].astype(BF16)
    w_sc = w_in[:, :, SSD_COLS:].astype(BF16)
    w_dt = jnp.tile(w_in[:, :, dt_lo:SSD_COLS].astype(BF16), (1, 1, LANES // HEAD_DIRS))
    w_out_b = w_out.astype(BF16)
    w_down_b = w_down.astype(BF16)

    h_ctx = ctx.reshape(1, n_ctx_rows, d)
    mx, mc = mod_vectors(0)
    hx = norm_mod(x, mix_norm_g[0], mx[0], mx[1])
    hc = norm_mod(h_ctx, mix_norm_g[0], mc[0], mc[1])
    zero_state = jnp.zeros((batch, SSD_STATE, SSD_WIDTH), F32)
    zeros_d = jnp.zeros((batch, 1, d), F32)
    out = None

    for l in range(depth):
        last = l == depth - 1
        layer = (ssd_conv_w[l], ssd_conv_b[l], ssd_dt_bias[l], ssd_a_log[l], ssd_d[l],
                 ssd_norm_g[l], sc_conv_w[l])
        if not last:
            mx_next, mc_next = mod_vectors(l + 1)

        if last:
            _, _, state_f, state_b = _mixer(hc, batch, w_ssd, w_dt, None, l, layer,
                                            zero_state, zero_state)
        else:
            y_ssd, y_sc, state_f, state_b = _mixer(hc, batch, w_ssd, w_dt, w_sc, l, layer,
                                                   zero_state, zero_state)
            h_ctx, hf = out_proj(y_ssd.reshape(1, n_ctx_rows, -1), y_sc.reshape(1, n_ctx_rows, -1),
                                 w_out_b, l, h_ctx, mc[2], ffn_norm_g[l], mc[3], mc[4])
            act = ffn_up(hf, w_gate, w_up, l)
            h_ctx, hc = ffn_down(act, w_down_b, l, h_ctx, mc[5], mix_norm_g[l + 1],
                                 mc_next[0], mc_next[1], emit_x=True, h_dtype=BF16)

        col_major = l % 2 == 1
        if col_major:
            hx = _to_col_major(hx, rows)
        y_ssd, y_sc, _, _ = _mixer(hx, batch, w_ssd, w_dt, w_sc, l, layer, state_f, state_b)
        if col_major:
            y_ssd = _from_col_major(y_ssd, rows)
            y_sc = _from_col_major(y_sc, rows)
        x, hf = out_proj(y_ssd, y_sc, w_out_b, l, x, mx[2], ffn_norm_g[l], mx[3], mx[4])
        act = ffn_up(hf, w_gate, w_up, l)
        if last:
            (out,) = ffn_down(act, w_down_b, l, x, mx[5], final_norm_g, zeros_d, zeros_d,
                              emit_x=False, h_dtype=x.dtype)
        else:
            x, hx = ffn_down(act, w_down_b, l, x, mx[5], mix_norm_g[l + 1],
                             mx_next[0], mx_next[1], emit_x=True, h_dtype=BF16)
            mx, mc = mx_next, mc_next
    return out
```

```python
import functools

import jax
import jax.numpy as jnp
import numpy as np
from jax import lax
from jax.experimental import pallas as pl
from jax.experimental.pallas import tpu as pltpu

F32 = jnp.float32
BF16 = jnp.bfloat16

D_MODEL = 2048
GRID_W = 64
NORM_EPS = 1e-6
SSD_WIDTH = 1024
SSD_HEAD_DIM = 64
SSD_HEADS = 16
SSD_GROUPS = 2
SSD_STATE = 128
SSD_CONV_W = 5
SSD_CHUNK = 128
SSD_CHUNKS_PER_STEP = 8
SSD_GN = SSD_GROUPS * SSD_STATE
SSD_XBC = SSD_WIDTH + 2 * SSD_GN
SSD_COLS = SSD_WIDTH + SSD_XBC + 2 * SSD_HEADS
SC_WIDTH = 1024
SC_CONV_W = 3
FFN_HIDDEN = 5632
LANES = 128
HEAD_GROUP_COLS = SSD_WIDTH // SSD_GROUPS
VMEM_LIMIT = 56 * 1024 * 1024
FFN_DOWN_VMEM_LIMIT = 62 * 1024 * 1024
LOG2_E = 1.4426950408889634


def _params(*sem, vmem=VMEM_LIMIT):
    return pltpu.CompilerParams(dimension_semantics=sem, vmem_limit_bytes=vmem)


def _silu(v):
    return v * jax.nn.sigmoid(v)


def _softplus(v):
    return jnp.maximum(v, 0.0) + jnp.log1p(jnp.exp(-jnp.abs(v)))


def _norm_rows(v, g):
    ms = jnp.mean(v * v, axis=-1, keepdims=True)
    return v * lax.rsqrt(ms + NORM_EPS) * g


def _dot(a, b):
    return jnp.dot(a, b, preferred_element_type=F32)


def _adaln_kernel(c_ref, w_ref, b_ref, o_ref):
    s = _silu(c_ref[...]).astype(BF16)
    o_ref[0] = _dot(s, w_ref[0].astype(BF16)) + b_ref[0]


def adaln(cc, ada_w, ada_b, tn=1536):
    depth, d, n = ada_w.shape
    rows = cc.shape[0]
    return pl.pallas_call(
        _adaln_kernel,
        grid=(depth, n // tn),
        in_specs=[
            pl.BlockSpec((rows, d), lambda l, j: (0, 0)),
            pl.BlockSpec((1, d, tn), lambda l, j: (l, 0, j)),
            pl.BlockSpec((1, 1, tn), lambda l, j: (l, 0, j)),
        ],
        out_specs=pl.BlockSpec((1, rows, tn), lambda l, j: (l, 0, j)),
        out_shape=jax.ShapeDtypeStruct((depth, rows, n), F32),
        compiler_params=_params("arbitrary", "arbitrary"),
        name="adaln",
    )(cc, ada_w, ada_b.reshape(depth, 1, n))


def _norm_mod_kernel(x_ref, g_ref, sh_ref, sc_ref, o_ref):
    y = _norm_rows(x_ref[0], g_ref[...])
    o_ref[0] = (y * (1.0 + sc_ref[0]) + sh_ref[0]).astype(o_ref.dtype)


def norm_mod(x, g, shift, scale, tr=1024):
    nb, length, d = x.shape
    return pl.pallas_call(
        _norm_mod_kernel,
        grid=(nb, length // tr),
        in_specs=[
            pl.BlockSpec((1, tr, d), lambda b, i: (b, i, 0)),
            pl.BlockSpec((1, d), lambda b, i: (0, 0)),
            pl.BlockSpec((1, 1, d), lambda b, i: (b, 0, 0)),
            pl.BlockSpec((1, 1, d), lambda b, i: (b, 0, 0)),
        ],
        out_specs=pl.BlockSpec((1, tr, d), lambda b, i: (b, i, 0)),
        out_shape=jax.ShapeDtypeStruct((nb, length, d), BF16),
        compiler_params=_params("arbitrary", "arbitrary"),
        name="norm_mod",
    )(x, g.reshape(1, d), shift, scale)


def _matmul_kernel(a_ref, w_ref, o_ref, *, sub):
    a = a_ref[0]
    tn = o_ref.shape[-1]
    for n0 in range(0, tn, sub):
        o_ref[0, :, n0:n0 + sub] = _dot(a, w_ref[0, :, n0:n0 + sub]).astype(o_ref.dtype)


def matmul(a, w, layer, out_dtype, tm, tn, sub):
    nb, length, k = a.shape
    n = w.shape[-1]
    return pl.pallas_call(
        functools.partial(_matmul_kernel, sub=sub),
        grid=(n // tn, nb, length // tm),
        in_specs=[
            pl.BlockSpec((1, tm, k), lambda j, b, i: (b, i, 0)),
            pl.BlockSpec((1, k, tn), lambda j, b, i: (layer, 0, j)),
        ],
        out_specs=pl.BlockSpec((1, tm, tn), lambda j, b, i: (b, i, j)),
        out_shape=jax.ShapeDtypeStruct((nb, length, n), out_dtype),
        compiler_params=_params("arbitrary", "arbitrary", "arbitrary"),
        name="matmul",
    )(a, w)


def _in_proj_ssd_kernel(a_ref, w_ref, wdt_ref, o_ref, dt_ref, *, sub):
    a = a_ref[0]
    for n0 in range(0, o_ref.shape[-1], sub):
        o_ref[0, :, n0:n0 + sub] = _dot(a, w_ref[0, :, n0:n0 + sub]).astype(o_ref.dtype)
    dt_ref[0] = _dot(a, wdt_ref[0])


def in_proj_ssd(a, w, w_dt, layer, tm, sub=256):
    nb, length, k = a.shape
    n = w.shape[-1]
    tok = lambda b, i: (b, i, 0)
    of_layer = lambda b, i: (layer, 0, 0)
    return pl.pallas_call(
        functools.partial(_in_proj_ssd_kernel, sub=sub),
        grid=(nb, length // tm),
        in_specs=[pl.BlockSpec((1, tm, k), tok), pl.BlockSpec((1, k, n), of_layer),
                  pl.BlockSpec((1, k, LANES), of_layer)],
        out_specs=[pl.BlockSpec((1, tm, n), tok), pl.BlockSpec((1, tm, LANES), tok)],
        out_shape=[jax.ShapeDtypeStruct((nb, length, n), BF16), jax.ShapeDtypeStruct((nb, length, LANES), F32)],
        compiler_params=_params("arbitrary", "arbitrary"),
        name="in_proj_ssd",
    )(a, w, w_dt)


def _shifted_taps(pad_ref, r0, rows, taps):
    half = taps // 2
    win = pad_ref[r0:r0 + rows + 32, :]
    out = []
    for k in range(taps):
        shift = (half - k) % (rows + 32)
        rolled = win if shift == 0 else pltpu.roll(win, shift, axis=0)
        out.append(rolled[16:16 + rows, :])
    return out


def _fill_padded(pad_ref, vals):
    t = vals.shape[0]
    zeros = jnp.zeros((16, vals.shape[1]), F32)
    pad_ref[0:16, :] = zeros
    pad_ref[t + 16:t + 32, :] = zeros
    pad_ref[16:t + 16, :] = vals


def _conv_xbc_kernel(x_ref, w_ref, b_ref, o_ref, pad_ref, *, rows):
    t = x_ref.shape[1]
    _fill_padded(pad_ref, x_ref[0].astype(F32))
    for r0 in range(0, t, rows):
        taps = _shifted_taps(pad_ref, r0, rows, SSD_CONV_W)
        acc = b_ref[...] + w_ref[0:1, :] * taps[0]
        for k in range(1, SSD_CONV_W):
            acc = acc + w_ref[k:k + 1, :] * taps[k]
        o_ref[0, r0:r0 + rows, :] = _silu(acc).astype(o_ref.dtype)


def conv_xbc(p, conv_w, conv_b, ct=256):
    nseq, t, _ = p.shape
    off = SSD_WIDTH // ct
    w8 = jnp.zeros((8, SSD_XBC), F32).at[:SSD_CONV_W].set(conv_w)
    return pl.pallas_call(
        functools.partial(_conv_xbc_kernel, rows=min(t, 256)),
        grid=(nseq, SSD_XBC // ct),
        in_specs=[
            pl.BlockSpec((1, t, ct), lambda s, j: (s, 0, off + j)),
            pl.BlockSpec((8, ct), lambda s, j: (0, j)),
            pl.BlockSpec((1, ct), lambda s, j: (0, j)),
        ],
        out_specs=pl.BlockSpec((1, t, ct), lambda s, j: (s, 0, j)),
        out_shape=jax.ShapeDtypeStruct((nseq, t, SSD_XBC), BF16),
        scratch_shapes=[pltpu.VMEM((t + 32, ct), F32)],
        compiler_params=_params("arbitrary", "arbitrary"),
        name="conv_xbc",
    )(p, w8, conv_b.reshape(1, SSD_XBC))


def _conv_sc_kernel(gb_ref, gc_ref, v_ref, w_ref, o_ref, pad_ref, *, rows):
    t = v_ref.shape[1]
    _fill_padded(pad_ref, gc_ref[0].astype(F32) * v_ref[0].astype(F32))
    for r0 in range(0, t, rows):
        taps = _shifted_taps(pad_ref, r0, rows, SC_CONV_W)
        acc = w_ref[0:1, :] * taps[0]
        for k in range(1, SC_CONV_W):
            acc = acc + w_ref[k:k + 1, :] * taps[k]
        gate = gb_ref[0, r0:r0 + rows, :].astype(F32)
        o_ref[0, r0:r0 + rows, :] = (gate * acc).astype(o_ref.dtype)


def conv_sc(p, sc_w, ct=512):
    nseq, t, _ = p.shape
    base = 0
    step = SC_WIDTH // ct
    w8 = jnp.zeros((8, SC_WIDTH), F32).at[:SC_CONV_W].set(sc_w)
    col = lambda k: pl.BlockSpec((1, t, ct), lambda s, j: (s, 0, base + k * step + j))
    return pl.pallas_call(
        functools.partial(_conv_sc_kernel, rows=min(t, 256)),
        grid=(nseq, SC_WIDTH // ct),
        in_specs=[col(0), col(1), col(2), pl.BlockSpec((8, ct), lambda s, j: (0, j))],
        out_specs=pl.BlockSpec((1, t, ct), lambda s, j: (s, 0, j)),
        out_shape=jax.ShapeDtypeStruct((nseq, t, SC_WIDTH), BF16),
        scratch_shapes=[pltpu.VMEM((t + 32, ct), F32)],
        compiler_params=_params("arbitrary", "arbitrary"),
        name="conv_sc",
    )(p, p, p, w8)


def _split3(v):
    hi = v.astype(BF16)
    r1 = v - hi.astype(F32)
    mid = r1.astype(BF16)
    return hi, mid, (r1 - mid.astype(F32)).astype(BF16)


def _cumsums(v, tri):
    return _dot(tri, jnp.concatenate(_split3(v), axis=0))


HEAD_DIRS = 2 * SSD_HEADS
SEL_ROWS = 2 * LANES
SEL_SPREAD = HEAD_DIRS * LANES
SEL_FWD = SEL_SPREAD + SSD_WIDTH
SEL_COLS = SEL_FWD + 2 * SSD_WIDTH


def _selection_matrix():
    s = np.zeros((SEL_ROWS, SEL_COLS), np.float32)
    for band in range(3):
        base = band * 2 * HEAD_DIRS
        for k in range(HEAD_DIRS):
            s[base + k, k * LANES:(k + 1) * LANES] = 1.0
        for h in range(SSD_HEADS):
            ch = slice(h * SSD_HEAD_DIM, (h + 1) * SSD_HEAD_DIM)
            s[base + HEAD_DIRS + h, SEL_SPREAD:SEL_FWD][ch] = 1.0
            s[base + SSD_HEADS + h, SEL_FWD:SEL_FWD + SSD_WIDTH][ch] = 1.0
            s[base + HEAD_DIRS + SSD_HEADS + h, SEL_FWD + SSD_WIDTH:SEL_COLS][ch] = 1.0
    return jnp.asarray(s, BF16)


def _packed_operand(acs, dt):
    lane = lax.broadcasted_iota(jnp.int32, acs.shape, 1)
    packed = jnp.where((lane & (2 * HEAD_DIRS - 1)) < HEAD_DIRS, acs, dt)
    hi, mid, lo = _split3(packed)
    first = jnp.where(lane < 2 * HEAD_DIRS, hi.astype(F32), mid.astype(F32)).astype(BF16)
    return jnp.concatenate([first, lo], axis=1)


def _state_update(st_ref, b_mat, xw, decay_row):
    for g in range(SSD_GROUPS):
        cols = slice(g * HEAD_GROUP_COLS, (g + 1) * HEAD_GROUP_COLS)
        contrib = lax.dot_general(
            b_mat[:, g * SSD_STATE:(g + 1) * SSD_STATE], xw[:, cols],
            (((0,), (0,)), ((), ())), preferred_element_type=F32)
        st_ref[:, cols] = st_ref[:, cols] * decay_row[:, cols] + contrib


def _ssd_kernel(xbc_ref, dt_ref, z_ref, h0f_ref, h0b_ref, dtb_ref, alog_ref, dsk_ref, g_ref, tri_ref,
                sel_ref, y_ref, hfl_ref, hbl_ref, hbe_ref, stf_ref, stb_ref, *, ng, per_step):
    q = SSD_CHUNK
    j = pl.program_id(1)
    fwd_lanes = 0
    bwd_lanes = SSD_HEADS

    def chunk_inputs(rows):
        dt = _softplus(dt_ref[0, rows, :] + dtb_ref[...])
        d_a = dt * (-jnp.exp(alog_ref[...]) * LOG2_E)
        xs = xbc_ref[0, rows, 0:SSD_WIDTH].astype(F32)
        b_mat = xbc_ref[0, rows, SSD_WIDTH:SSD_WIDTH + SSD_GN]
        return dt, d_a, xs, b_mat

    @pl.when(j == 0)
    def _():
        stf_ref[...] = h0f_ref[0]
        stb_ref[...] = h0b_ref[0]

    def backward_chunk(rows, c):
        dt, d_a, xs, b_mat = chunk_inputs(rows)
        acs_b = _cumsums(d_a, tri_ref[q:2 * q, :])
        spread = _dot(_packed_operand(acs_b, dt), sel_ref[:, SEL_FWD:SEL_COLS])
        sum_b = spread[:, 0:SSD_WIDTH]
        tot = sum_b[0:1, :]
        w_full = spread[:, SSD_WIDTH:2 * SSD_WIDTH] * jnp.exp2(tot - sum_b)
        xw = (xs * w_full).astype(BF16)
        hbe_ref[c] = stb_ref[...].astype(BF16)
        _state_update(stb_ref, b_mat, xw, jnp.exp2(tot))

    @pl.when(j < ng)
    def _():
        for sub in reversed(range(per_step)):
            backward_chunk(slice(sub * q, (sub + 1) * q), (ng - 1 - j) * per_step + sub)

        @pl.when(j == ng - 1)
        def _():
            hbl_ref[0] = stb_ref[...]

    def forward_chunk(rows, c):
        dt, d_a, xs, b_mat = chunk_inputs(rows)
        row = lax.broadcasted_iota(jnp.int32, (q, q), 0)
        lane = lax.broadcasted_iota(jnp.int32, (q, q), 1)
        sums = _cumsums(d_a, tri_ref[...])
        acs = jnp.where((lane & (HEAD_DIRS - 1)) < bwd_lanes, sums[0:q], sums[q:2 * q])
        spread = _dot(_packed_operand(acs, dt), sel_ref[:, 0:SEL_FWD])
        acs_t = acs.T
        dt_t = dt.T
        src_t = acs_t - jnp.log2(dt_t)
        both_t = jnp.log2(dt_t[fwd_lanes:fwd_lanes + SSD_HEADS] + dt_t[bwd_lanes:bwd_lanes + SSD_HEADS])
        c_mat = xbc_ref[0, rows, SSD_WIDTH + SSD_GN:SSD_XBC]
        below = row > lane
        diag = row == lane

        cb = []
        for g in range(SSD_GROUPS):
            gs = slice(g * SSD_STATE, (g + 1) * SSD_STATE)
            cb.append(lax.dot_general(c_mat[:, gs], b_mat[:, gs], (((1,), (1,)), ((), ())),
                                      preferred_element_type=F32))

        y_parts = []
        for i in range(SSD_HEADS // 2):
            mats = []
            for h in (2 * i, 2 * i + 1):
                hf, hb = fwd_lanes + h, bwd_lanes + h
                tgt = jnp.where(below, spread[:, hf * LANES:(hf + 1) * LANES],
                                spread[:, hb * LANES:(hb + 1) * LANES])
                src = jnp.where(below, src_t[hf:hf + 1, :], src_t[hb:hb + 1, :])
                arg = jnp.where(diag, both_t[h:h + 1, :], tgt - src)
                mats.append((cb[h // (SSD_HEADS // SSD_GROUPS)] * jnp.exp2(arg)).astype(BF16))
            pair = xs[:, i * LANES:(i + 1) * LANES]
            top = jnp.where(lane < SSD_HEAD_DIM, pair, 0.0).astype(BF16)
            bot = jnp.where(lane >= SSD_HEAD_DIM, pair, 0.0).astype(BF16)
            y_parts.append(_dot(jnp.concatenate(mats, axis=1), jnp.concatenate([top, bot], axis=0)))
        y = jnp.concatenate(y_parts, axis=1)

        half = (lax.broadcasted_iota(jnp.int32, (q, SSD_WIDTH), 1) & (LANES - 1)) < SSD_HEAD_DIM

        def per_channel(first_block):
            even = [spread[:, (first_block + 2 * i) * LANES:(first_block + 2 * i + 1) * LANES]
                    for i in range(SSD_HEADS // 2)]
            odd = [spread[:, (first_block + 2 * i + 1) * LANES:(first_block + 2 * i + 2) * LANES]
                   for i in range(SSD_HEADS // 2)]
            return jnp.where(half, jnp.concatenate(even, axis=1), jnp.concatenate(odd, axis=1))

        sum_f = per_channel(fwd_lanes)
        e_f = jnp.exp2(sum_f)
        e_b = jnp.exp2(per_channel(bwd_lanes))
        st_f = stf_ref[...].astype(BF16)
        st_b = hbe_ref[c]
        off_f, off_b = [], []
        for g in range(SSD_GROUPS):
            gs = slice(g * SSD_STATE, (g + 1) * SSD_STATE)
            cols = slice(g * HEAD_GROUP_COLS, (g + 1) * HEAD_GROUP_COLS)
            off_f.append(_dot(c_mat[:, gs], st_f[:, cols]))
            off_b.append(_dot(c_mat[:, gs], st_b[:, cols]))
        y = y + e_f * jnp.concatenate(off_f, axis=1) + e_b * jnp.concatenate(off_b, axis=1)
        y = y + dsk_ref[...] * xs

        w_full = spread[:, SEL_SPREAD:SEL_FWD] * jnp.exp2(sum_f[q - 1:q, :] - sum_f)
        _state_update(stf_ref, b_mat, (xs * w_full).astype(BF16), e_f[q - 1:q, :])

        z = z_ref[0, rows, :].astype(F32)
        y_ref[0, rows, :] = _norm_rows(y * _silu(z), g_ref[...]).astype(y_ref.dtype)

    @pl.when(j >= ng)
    def _():
        for sub in range(per_step):
            forward_chunk(slice(sub * q, (sub + 1) * q), (j - ng) * per_step + sub)

        @pl.when(j == 2 * ng - 1)
        def _():
            hfl_ref[0] = stf_ref[...]


def ssd(xbc, dt_raw, p, h0_f, h0_b, dt_bias, a_log, d_skip, norm_g):
    nseq, t, _ = xbc.shape
    nc = t // SSD_CHUNK
    per_step = min(SSD_CHUNKS_PER_STEP, nc)
    ng = nc // per_step
    rows = per_step * SSD_CHUNK
    pad_row = lambda v: jnp.tile(v.reshape(1, HEAD_DIRS), (1, LANES // HEAD_DIRS))
    ones = np.ones((SSD_CHUNK, SSD_CHUNK), np.float32)
    tri = jnp.asarray(np.concatenate([np.tile(np.tril(ones), (1, 3)), np.tile(np.triu(ones), (1, 3))]), BF16)
    chunk = lambda s, j: (s, jnp.where(j < ng, ng - 1 - j, j - ng), 0)
    out_chunk = lambda s, j: (s, jnp.maximum(j - ng, 0), 0)
    const2 = lambda s, j: (0, 0)
    seq = lambda s, j: (s, 0, 0)
    state_shape = jax.ShapeDtypeStruct((nseq, SSD_STATE, SSD_WIDTH), F32)
    return pl.pallas_call(
        functools.partial(_ssd_kernel, ng=ng, per_step=per_step),
        grid=(nseq, 2 * ng),
        in_specs=[
            pl.BlockSpec((1, rows, SSD_XBC), chunk),
            pl.BlockSpec((1, rows, LANES), chunk),
            pl.BlockSpec((1, rows, SSD_WIDTH), out_chunk),
            pl.BlockSpec((1, SSD_STATE, SSD_WIDTH), seq),
            pl.BlockSpec((1, SSD_STATE, SSD_WIDTH), seq),
            pl.BlockSpec((1, LANES), const2),
            pl.BlockSpec((1, LANES), const2),
            pl.BlockSpec((1, SSD_WIDTH), const2),
            pl.BlockSpec((1, SSD_WIDTH), const2),
            pl.BlockSpec((2 * SSD_CHUNK, 3 * SSD_CHUNK), const2),
            pl.BlockSpec((SEL_ROWS, SEL_COLS), const2),
        ],
        out_specs=[
            pl.BlockSpec((1, rows, SSD_WIDTH), out_chunk),
            pl.BlockSpec((1, SSD_STATE, SSD_WIDTH), seq),
            pl.BlockSpec((1, SSD_STATE, SSD_WIDTH), seq),
        ],
        out_shape=[jax.ShapeDtypeStruct((nseq, t, SSD_WIDTH), BF16), state_shape, state_shape],
        scratch_shapes=[
            pltpu.VMEM((nc, SSD_STATE, SSD_WIDTH), BF16),
            pltpu.VMEM((SSD_STATE, SSD_WIDTH), F32),
            pltpu.VMEM((SSD_STATE, SSD_WIDTH), F32),
        ],
        compiler_params=_params("arbitrary", "arbitrary"),
        name="ssd",
    )(xbc, dt_raw, p, h0_f, h0_b, pad_row(dt_bias), pad_row(a_log),
      jnp.repeat(d_skip, SSD_HEAD_DIM).reshape(1, SSD_WIDTH), norm_g.reshape(1, SSD_WIDTH), tri,
      _selection_matrix())


def _out_proj_kernel(y1_ref, y2_ref, w_ref, x_ref, gate_ref, g_ref, sh_ref, sc_ref,
                     xo_ref, h_ref, *, sub):
    y = jnp.concatenate([y1_ref[0], y2_ref[0]], axis=1)
    n = xo_ref.shape[-1]
    for n0 in range(0, n, sub):
        ns = slice(n0, n0 + sub)
        xo_ref[0, :, ns] = x_ref[0, :, ns] + gate_ref[0, :, ns] * _dot(y, w_ref[0, :, ns])
    hn = _norm_rows(xo_ref[0], g_ref[...])
    h_ref[0] = (hn * (1.0 + sc_ref[0]) + sh_ref[0]).astype(h_ref.dtype)


def out_proj(y_ssd, y_sc, w_out, layer, x, gate, g, shift, scale, tm=512, sub=1024):
    nb, length, d = x.shape
    k1, k2 = y_ssd.shape[-1], y_sc.shape[-1]
    tok = lambda b, i: (b, i, 0)
    per_b = lambda b, i: (b, 0, 0)
    return pl.pallas_call(
        functools.partial(_out_proj_kernel, sub=sub),
        grid=(nb, length // tm),
        in_specs=[
            pl.BlockSpec((1, tm, k1), tok),
            pl.BlockSpec((1, tm, k2), tok),
            pl.BlockSpec((1, k1 + k2, d), lambda b, i: (layer, 0, 0)),
            pl.BlockSpec((1, tm, d), tok),
            pl.BlockSpec((1, 1, d), per_b),
            pl.BlockSpec((1, d), lambda b, i: (0, 0)),
            pl.BlockSpec((1, 1, d), per_b),
            pl.BlockSpec((1, 1, d), per_b),
        ],
        out_specs=[pl.BlockSpec((1, tm, d), tok), pl.BlockSpec((1, tm, d), tok)],
        out_shape=[jax.ShapeDtypeStruct((nb, length, d), F32),
                   jax.ShapeDtypeStruct((nb, length, d), BF16)],
        compiler_params=_params("arbitrary", "arbitrary"),
        name="out_proj",
    )(y_ssd, y_sc, w_out, x, gate, g.reshape(1, d), shift, scale)


def _ffn_up_kernel(h_ref, wg_ref, wu_ref, o_ref, wgb_ref, wub_ref, *, sub_m):
    @pl.when((pl.program_id(1) == 0) & (pl.program_id(2) == 0))
    def _():
        wgb_ref[...] = wg_ref[0].astype(BF16)
        wub_ref[...] = wu_ref[0].astype(BF16)

    tm = h_ref.shape[1]
    for m0 in range(0, tm, sub_m):
        h = h_ref[0, m0:m0 + sub_m, :]
        o_ref[0, m0:m0 + sub_m, :] = (_silu(_dot(h, wgb_ref[...])) * _dot(h, wub_ref[...])).astype(o_ref.dtype)


def ffn_up(h, w_gate, w_up, layer, tm=2048, tn=512, sub_m=1024):
    nb, length, d = h.shape
    n = w_gate.shape[-1]
    tm = min(tm, length)
    w_spec = pl.BlockSpec((1, d, tn), lambda j, b, i: (layer, 0, j))
    return pl.pallas_call(
        functools.partial(_ffn_up_kernel, sub_m=sub_m),
        grid=(n // tn, nb, length // tm),
        in_specs=[pl.BlockSpec((1, tm, d), lambda j, b, i: (b, i, 0)), w_spec, w_spec],
        out_specs=pl.BlockSpec((1, tm, tn), lambda j, b, i: (b, i, j)),
        out_shape=jax.ShapeDtypeStruct((nb, length, n), BF16),
        scratch_shapes=[pltpu.VMEM((d, tn), BF16), pltpu.VMEM((d, tn), BF16)],
        compiler_params=_params("arbitrary", "arbitrary", "arbitrary"),
        name="ffn_up",
    )(h, w_gate, w_up)


def _ffn_down_kernel(a_ref, w_ref, x_ref, gate_ref, g_ref, sh_ref, sc_ref, *refs, sub_n):
    res_ref, h_ref = refs[0], refs[-1]
    n = res_ref.shape[2]
    a = a_ref[0]
    for n0 in range(0, n, sub_n):
        ns = slice(n0, n0 + sub_n)
        res_ref[0, :, ns] = x_ref[0, :, ns] + gate_ref[0, :, ns] * _dot(a, w_ref[0, :, ns])
    hn = _norm_rows(res_ref[0], g_ref[...])
    h_ref[0] = (hn * (1.0 + sc_ref[0]) + sh_ref[0]).astype(h_ref.dtype)


def ffn_down(act, w_down, layer, x, gate, g, shift, scale, emit_x, h_dtype, tm=512):
    nb, length, d = x.shape
    kk = act.shape[-1]
    tok = lambda b, i: (b, i, 0)
    per_b = lambda b, i: (b, 0, 0)
    if emit_x:
        out_shape = [jax.ShapeDtypeStruct((nb, length, d), F32), jax.ShapeDtypeStruct((nb, length, d), h_dtype)]
    else:
        assert h_dtype == F32
        out_shape = [jax.ShapeDtypeStruct((nb, length, d), F32)]
    return pl.pallas_call(
        functools.partial(_ffn_down_kernel, sub_n=d // 2),
        grid=(nb, length // tm),
        in_specs=[
            pl.BlockSpec((1, tm, kk), tok),
            pl.BlockSpec((1, kk, d), lambda b, i: (layer, 0, 0), pipeline_mode=pl.Buffered(1)),
            pl.BlockSpec((1, tm, d), tok),
            pl.BlockSpec((1, 1, d), per_b),
            pl.BlockSpec((1, d), lambda b, i: (0, 0)),
            pl.BlockSpec((1, 1, d), per_b),
            pl.BlockSpec((1, 1, d), per_b),
        ],
        out_specs=[pl.BlockSpec((1, tm, d), tok)] * len(out_shape),
        out_shape=out_shape,
        compiler_params=_params("arbitrary", "arbitrary", vmem=FFN_DOWN_VMEM_LIMIT),
        name="ffn_down",
    )(act, w_down, x, gate, g.reshape(1, d), shift, scale)


def _to_col_major(t, rows):
    b, length, ch = t.shape
    return t.reshape(b, rows, GRID_W, ch).transpose(0, 2, 1, 3).reshape(b, length, ch)


def _from_col_major(t, rows):
    b, length, ch = t.shape
    return t.reshape(b, GRID_W, rows, ch).transpose(0, 2, 1, 3).reshape(b, length, ch)


def _mixer(h_rows, nseq, w_ssd, w_dt, w_sc, l, layer, h0_f, h0_b):
    (conv_w, conv_b, dt_bias, a_log, d_skip, ssd_g, sc_w) = layer
    nb, length, _ = h_rows.shape
    t = nb * length // nseq
    tm = min(length, 1024)
    p, dt_raw = in_proj_ssd(h_rows, w_ssd, w_dt, l, tm=tm)
    p = p.reshape(nseq, t, p.shape[-1])
    dt_raw = dt_raw.reshape(nseq, t, LANES)
    xbc = conv_xbc(p, conv_w, conv_b)
    y_ssd, h_f, h_b = ssd(xbc, dt_raw, p, h0_f, h0_b, dt_bias, a_log, d_skip, ssd_g)
    y_sc = None
    if w_sc is not None:
        n_sc = w_sc.shape[-1]
        p_sc = matmul(h_rows, w_sc, l, BF16, tm=tm, tn=n_sc, sub=256)
        y_sc = conv_sc(p_sc.reshape(nseq, t, n_sc), sc_w)
    return y_ssd, y_sc, h_f, h_b


def kernel(x, c, ctx, c_ctx, ada_w, ada_b, mix_norm_g, w_in, ssd_conv_w, ssd_conv_b, ssd_dt_bias,
           ssd_a_log, ssd_d, ssd_norm_g, sc_conv_w, w_out, ffn_norm_g, w_gate, w_up, w_down,
           final_norm_g):
    batch, seq, d = x.shape
    depth = ada_w.shape[0]
    ctx_len = ctx.shape[1]
    rows = seq // GRID_W
    n_ctx_rows = batch * ctx_len

    cc = jnp.zeros((16, d), F32).at[:batch].set(c).at[batch].set(c_ctx)
    mods = adaln(cc, ada_w, ada_b)

    def mod_vectors(l):
        mx = [mods[l, :batch, i * d:(i + 1) * d].reshape(batch, 1, d) for i in range(6)]
        mc = [mods[l, batch:batch + 1, i * d:(i + 1) * d].reshape(1, 1, d) for i in range(6)]
        return mx, mc

    dt_lo = SSD_WIDTH + SSD_XBC
    w_ssd = w_in[:, :, :dt_lo].astype(BF16)
    w_sc = w_in[:, :, SSD_COLS:].astype(BF16)
    w_dt = jnp.tile(w_in[:, :, dt_lo:SSD_COLS].astype(BF16), (1, 1, LANES // HEAD_DIRS))
    w_out_b = w_out.astype(BF16)
    w_down_b = w_down.astype(BF16)

    h_ctx = ctx.reshape(1, n_ctx_rows, d)
    mx, mc = mod_vectors(0)
    hx = norm_mod(x, mix_norm_g[0], mx[0], mx[1])
    hc = norm_mod(h_ctx, mix_norm_g[0], mc[0], mc[1])
    zero_state = jnp.zeros((batch, SSD_STATE, SSD_WIDTH), F32)
    zeros_d = jnp.zeros((batch, 1, d), F32)
    out = None

    for l in range(depth):
        last = l == depth - 1
        layer = (ssd_conv_w[l], ssd_conv_b[l], ssd_dt_bias[l], ssd_a_log[l], ssd_d[l],
                 ssd_norm_g[l], sc_conv_w[l])
        if not last:
            mx_next, mc_next = mod_vectors(l + 1)

        if last:
            _, _, state_f, state_b = _mixer(hc, batch, w_ssd, w_dt, None, l, layer,
                                            zero_state, zero_state)
        else:
            y_ssd, y_sc, state_f, state_b = _mixer(hc, batch, w_ssd, w_dt, w_sc, l, layer,
                                                   zero_state, zero_state)
            h_ctx, hf = out_proj(y_ssd.reshape(1, n_ctx_rows, -1), y_sc.reshape(1, n_ctx_rows, -1),
                                 w_out_b, l, h_ctx, mc[2], ffn_norm_g[l], mc[3], mc[4])
            act = ffn_up(hf, w_gate, w_up, l)
            h_ctx, hc = ffn_down(act, w_down_b, l, h_ctx, mc[5], mix_norm_g[l + 1],
                                 mc_next[0], mc_next[1], emit_x=True, h_dtype=BF16)

        col_major = l % 2 == 1
        if col_major:
            hx = _to_col_major(hx, rows)
        y_ssd, y_sc, _, _ = _mixer(hx, batch, w_ssd, w_dt, w_sc, l, layer, state_f, state_b)
        if col_major:
            y_ssd = _from_col_major(y_ssd, rows)
            y_sc = _from_col_major(y_sc, rows)
        x, hf = out_proj(y_ssd, y_sc, w_out_b, l, x, mx[2], ffn_norm_g[l], mx[3], mx[4])
        act = ffn_up(hf, w_gate, w_up, l)
        if last:
            (out,) = ffn_down(act, w_down_b, l, x, mx[5], final_norm_g, zeros_d, zeros_d,
                              emit_x=False, h_dtype=x.dtype)
        else:
            x, hx = ffn_down(act, w_down_b, l, x, mx[5], mix_norm_g[l + 1],
                             mx_next[0], mx_next[1], emit_x=True, h_dtype=BF16)
            mx, mc = mx_next, mc_next
    return out
```

```python
import functools

import jax
import jax.numpy as jnp
import numpy as np
from jax import lax
from jax.experimental import pallas as pl
from jax.experimental.pallas import tpu as pltpu

F32 = jnp.float32
BF16 = jnp.bfloat16

D_MODEL = 2048
GRID_W = 64
NORM_EPS = 1e-6
SSD_WIDTH = 1024
SSD_HEAD_DIM = 64
SSD_HEADS = 16
SSD_GROUPS = 2
SSD_STATE = 128
SSD_CONV_W = 5
SSD_CHUNK = 128
SSD_CHUNKS_PER_STEP = 8
SSD_GN = SSD_GROUPS * SSD_STATE
SSD_XBC = SSD_WIDTH + 2 * SSD_GN
SSD_COLS = SSD_WIDTH + SSD_XBC + 2 * SSD_HEADS
SC_WIDTH = 1024
SC_CONV_W = 3
FFN_HIDDEN = 5632
LANES = 128
HEAD_GROUP_COLS = SSD_WIDTH // SSD_GROUPS
VMEM_LIMIT = 56 * 1024 * 1024
FFN_DOWN_VMEM_LIMIT = 62 * 1024 * 1024
LOG2_E = 1.4426950408889634


def _params(*sem, vmem=VMEM_LIMIT):
    return pltpu.CompilerParams(dimension_semantics=sem, vmem_limit_bytes=vmem)


def _silu(v):
    return v * jax.nn.sigmoid(v)


def _softplus(v):
    return jnp.maximum(v, 0.0) + jnp.log1p(jnp.exp(-jnp.abs(v)))


def _norm_rows(v, g):
    ms = jnp.mean(v * v, axis=-1, keepdims=True)
    return v * lax.rsqrt(ms + NORM_EPS) * g


def _dot(a, b):
    return jnp.dot(a, b, preferred_element_type=F32)


def _adaln_kernel(c_ref, w_ref, b_ref, o_ref):
    s = _silu(c_ref[...]).astype(BF16)
    o_ref[0] = _dot(s, w_ref[0].astype(BF16)) + b_ref[0]


def adaln(cc, ada_w, ada_b, tn=1536):
    depth, d, n = ada_w.shape
    rows = cc.shape[0]
    return pl.pallas_call(
        _adaln_kernel,
        grid=(depth, n // tn),
        in_specs=[
            pl.BlockSpec((rows, d), lambda l, j: (0, 0)),
            pl.BlockSpec((1, d, tn), lambda l, j: (l, 0, j)),
            pl.BlockSpec((1, 1, tn), lambda l, j: (l, 0, j)),
        ],
        out_specs=pl.BlockSpec((1, rows, tn), lambda l, j: (l, 0, j)),
        out_shape=jax.ShapeDtypeStruct((depth, rows, n), F32),
        compiler_params=_params("arbitrary", "arbitrary"),
        name="adaln",
    )(cc, ada_w, ada_b.reshape(depth, 1, n))


def _norm_mod_kernel(x_ref, g_ref, sh_ref, sc_ref, o_ref):
    y = _norm_rows(x_ref[0], g_ref[...])
    o_ref[0] = (y * (1.0 + sc_ref[0]) + sh_ref[0]).astype(o_ref.dtype)


def norm_mod(x, g, shift, scale, tr=1024):
    nb, length, d = x.shape
    return pl.pallas_call(
        _norm_mod_kernel,
        grid=(nb, length // tr),
        in_specs=[
            pl.BlockSpec((1, tr, d), lambda b, i: (b, i, 0)),
            pl.BlockSpec((1, d), lambda b, i: (0, 0)),
            pl.BlockSpec((1, 1, d), lambda b, i: (b, 0, 0)),
            pl.BlockSpec((1, 1, d), lambda b, i: (b, 0, 0)),
        ],
        out_specs=pl.BlockSpec((1, tr, d), lambda b, i: (b, i, 0)),
        out_shape=jax.ShapeDtypeStruct((nb, length, d), BF16),
        compiler_params=_params("arbitrary", "arbitrary"),
        name="norm_mod",
    )(x, g.reshape(1, d), shift, scale)


def _matmul_kernel(a_ref, w_ref, o_ref, *, sub):
    a = a_ref[0]
    tn = o_ref.shape[-1]
    for n0 in range(0, tn, sub):
        o_ref[0, :, n0:n0 + sub] = _dot(a, w_ref[0, :, n0:n0 + sub]).astype(o_ref.dtype)


def matmul(a, w, layer, out_dtype, tm, tn, sub):
    nb, length, k = a.shape
    n = w.shape[-1]
    return pl.pallas_call(
        functools.partial(_matmul_kernel, sub=sub),
        grid=(n // tn, nb, length // tm),
        in_specs=[
            pl.BlockSpec((1, tm, k), lambda j, b, i: (b, i, 0)),
            pl.BlockSpec((1, k, tn), lambda j, b, i: (layer, 0, j)),
        ],
        out_specs=pl.BlockSpec((1, tm, tn), lambda j, b, i: (b, i, j)),
        out_shape=jax.ShapeDtypeStruct((nb, length, n), out_dtype),
        compiler_params=_params("arbitrary", "arbitrary", "arbitrary"),
        name="matmul",
    )(a, w)


def _in_proj_ssd_kernel(a_ref, w_ref, wdt_ref, o_ref, dt_ref, *, sub):
    a = a_ref[0]
    for n0 in range(0, o_ref.shape[-1], sub):
        o_ref[0, :, n0:n0 + sub] = _dot(a, w_ref[0, :, n0:n0 + sub]).astype(o_ref.dtype)
    dt_ref[0] = _dot(a, wdt_ref[0])


def in_proj_ssd(a, w, w_dt, layer, tm, sub=256):
    nb, length, k = a.shape
    n = w.shape[-1]
    tok = lambda b, i: (b, i, 0)
    of_layer = lambda b, i: (layer, 0, 0)
    return pl.pallas_call(
        functools.partial(_in_proj_ssd_kernel, sub=sub),
        grid=(nb, length // tm),
        in_specs=[pl.BlockSpec((1, tm, k), tok), pl.BlockSpec((1, k, n), of_layer),
                  pl.BlockSpec((1, k, LANES), of_layer)],
        out_specs=[pl.BlockSpec((1, tm, n), tok), pl.BlockSpec((1, tm, LANES), tok)],
        out_shape=[jax.ShapeDtypeStruct((nb, length, n), BF16), jax.ShapeDtypeStruct((nb, length, LANES), F32)],
        compiler_params=_params("arbitrary", "arbitrary"),
        name="in_proj_ssd",
    )(a, w, w_dt)


def _shifted_taps(pad_ref, r0, rows, taps):
    half = taps // 2
    win = pad_ref[r0:r0 + rows + 32, :]
    out = []
    for k in range(taps):
        shift = (half - k) % (rows + 32)
        rolled = win if shift == 0 else pltpu.roll(win, shift, axis=0)
        out.append(rolled[16:16 + rows, :])
    return out


def _fill_padded(pad_ref, vals):
    t = vals.shape[0]
    zeros = jnp.zeros((16, vals.shape[1]), F32)
    pad_ref[0:16, :] = zeros
    pad_ref[t + 16:t + 32, :] = zeros
    pad_ref[16:t + 16, :] = vals


def _conv_xbc_kernel(x_ref, w_ref, b_ref, o_ref, pad_ref, *, rows):
    t = x_ref.shape[1]
    _fill_padded(pad_ref, x_ref[0].astype(F32))
    for r0 in range(0, t, rows):
        taps = _shifted_taps(pad_ref, r0, rows, SSD_CONV_W)
        acc = b_ref[...] + w_ref[0:1, :] * taps[0]
        for k in range(1, SSD_CONV_W):
            acc = acc + w_ref[k:k + 1, :] * taps[k]
        o_ref[0, r0:r0 + rows, :] = _silu(acc).astype(o_ref.dtype)


def conv_xbc(p, conv_w, conv_b, ct=256):
    nseq, t, _ = p.shape
    off = SSD_WIDTH // ct
    w8 = jnp.zeros((8, SSD_XBC), F32).at[:SSD_CONV_W].set(conv_w)
    return pl.pallas_call(
        functools.partial(_conv_xbc_kernel, rows=min(t, 256)),
        grid=(nseq, SSD_XBC // ct),
        in_specs=[
            pl.BlockSpec((1, t, ct), lambda s, j: (s, 0, off + j)),
            pl.BlockSpec((8, ct), lambda s, j: (0, j)),
            pl.BlockSpec((1, ct), lambda s, j: (0, j)),
        ],
        out_specs=pl.BlockSpec((1, t, ct), lambda s, j: (s, 0, j)),
        out_shape=jax.ShapeDtypeStruct((nseq, t, SSD_XBC), BF16),
        scratch_shapes=[pltpu.VMEM((t + 32, ct), F32)],
        compiler_params=_params("arbitrary", "arbitrary"),
        name="conv_xbc",
    )(p, w8, conv_b.reshape(1, SSD_XBC))


def _conv_sc_kernel(gb_ref, gc_ref, v_ref, w_ref, o_ref, pad_ref, *, rows):
    t = v_ref.shape[1]
    _fill_padded(pad_ref, gc_ref[0].astype(F32) * v_ref[0].astype(F32))
    for r0 in range(0, t, rows):
        taps = _shifted_taps(pad_ref, r0, rows, SC_CONV_W)
        acc = w_ref[0:1, :] * taps[0]
        for k in range(1, SC_CONV_W):
            acc = acc + w_ref[k:k + 1, :] * taps[k]
        gate = gb_ref[0, r0:r0 + rows, :].astype(F32)
        o_ref[0, r0:r0 + rows, :] = (gate * acc).astype(o_ref.dtype)


def conv_sc(p, sc_w, ct=512):
    nseq, t, _ = p.shape
    base = 0
    step = SC_WIDTH // ct
    w8 = jnp.zeros((8, SC_WIDTH), F32).at[:SC_CONV_W].set(sc_w)
    col = lambda k: pl.BlockSpec((1, t, ct), lambda s, j: (s, 0, base + k * step + j))
    return pl.pallas_call(
        functools.partial(_conv_sc_kernel, rows=min(t, 256)),
        grid=(nseq, SC_WIDTH // ct),
        in_specs=[col(0), col(1), col(2), pl.BlockSpec((8, ct), lambda s, j: (0, j))],
        out_specs=pl.BlockSpec((1, t, ct), lambda s, j: (s, 0, j)),
        out_shape=jax.ShapeDtypeStruct((nseq, t, SC_WIDTH), BF16),
        scratch_shapes=[pltpu.VMEM((t + 32, ct), F32)],
        compiler_params=_params("arbitrary", "arbitrary"),
        name="conv_sc",
    )(p, p, p, w8)


def _split3(v):
    hi = v.astype(BF16)
    r1 = v - hi.astype(F32)
    mid = r1.astype(BF16)
    return hi, mid, (r1 - mid.astype(F32)).astype(BF16)


def _cumsums(v, tri):
    return _dot(tri, jnp.concatenate(_split3(v), axis=0))


HEAD_DIRS = 2 * SSD_HEADS
SEL_ROWS = 2 * LANES
SEL_SPREAD = HEAD_DIRS * LANES
SEL_FWD = SEL_SPREAD + SSD_WIDTH
SEL_COLS = SEL_FWD + 2 * SSD_WIDTH


def _selection_matrix():
    s = np.zeros((SEL_ROWS, SEL_COLS), np.float32)
    for band in range(3):
        base = band * 2 * HEAD_DIRS
        for k in range(HEAD_DIRS):
            s[base + k, k * LANES:(k + 1) * LANES] = 1.0
        for h in range(SSD_HEADS):
            ch = slice(h * SSD_HEAD_DIM, (h + 1) * SSD_HEAD_DIM)
            s[base + HEAD_DIRS + h, SEL_SPREAD:SEL_FWD][ch] = 1.0
            s[base + SSD_HEADS + h, SEL_FWD:SEL_FWD + SSD_WIDTH][ch] = 1.0
            s[base + HEAD_DIRS + SSD_HEADS + h, SEL_FWD + SSD_WIDTH:SEL_COLS][ch] = 1.0
    return jnp.asarray(s, BF16)


def _packed_operand(acs, dt):
    lane = lax.broadcasted_iota(jnp.int32, acs.shape, 1)
    packed = jnp.where((lane & (2 * HEAD_DIRS - 1)) < HEAD_DIRS, acs, dt)
    hi, mid, lo = _split3(packed)
    first = jnp.where(lane < 2 * HEAD_DIRS, hi.astype(F32), mid.astype(F32)).astype(BF16)
    return jnp.concatenate([first, lo], axis=1)


def _state_update(st_ref, b_mat, xw, decay_row):
    for g in range(SSD_GROUPS):
        cols = slice(g * HEAD_GROUP_COLS, (g + 1) * HEAD_GROUP_COLS)
        contrib = lax.dot_general(
            b_mat[:, g * SSD_STATE:(g + 1) * SSD_STATE], xw[:, cols],
            (((0,), (0,)), ((), ())), preferred_element_type=F32)
        st_ref[:, cols] = st_ref[:, cols] * decay_row[:, cols] + contrib


def _ssd_kernel(xbc_ref, dt_ref, z_ref, h0f_ref, h0b_ref, dtb_ref, alog_ref, dsk_ref, g_ref, tri_ref,
                sel_ref, *refs, ng, per_step, emit_y):
    y_ref = refs[0] if emit_y else None
    hfl_ref, hbl_ref, hbe_ref, stf_ref, stb_ref = refs[-5:]
    q = SSD_CHUNK
    j = pl.program_id(1)
    fwd_lanes = 0
    bwd_lanes = SSD_HEADS

    def chunk_inputs(rows):
        dt = _softplus(dt_ref[0, rows, :] + dtb_ref[...])
        d_a = dt * (-jnp.exp(alog_ref[...]) * LOG2_E)
        xs = xbc_ref[0, rows, 0:SSD_WIDTH].astype(F32)
        b_mat = xbc_ref[0, rows, SSD_WIDTH:SSD_WIDTH + SSD_GN]
        return dt, d_a, xs, b_mat

    @pl.when(j == 0)
    def _():
        stf_ref[...] = h0f_ref[0]
        stb_ref[...] = h0b_ref[0]

    def backward_chunk(rows, c):
        dt, d_a, xs, b_mat = chunk_inputs(rows)
        acs_b = _cumsums(d_a, tri_ref[q:2 * q, :])
        spread = _dot(_packed_operand(acs_b, dt), sel_ref[:, SEL_FWD:SEL_COLS])
        sum_b = spread[:, 0:SSD_WIDTH]
        tot = sum_b[0:1, :]
        w_full = spread[:, SSD_WIDTH:2 * SSD_WIDTH] * jnp.exp2(tot - sum_b)
        xw = (xs * w_full).astype(BF16)
        if emit_y:
            hbe_ref[c] = stb_ref[...].astype(BF16)
        _state_update(stb_ref, b_mat, xw, jnp.exp2(tot))

    @pl.when(j < ng)
    def _():
        for sub in reversed(range(per_step)):
            backward_chunk(slice(sub * q, (sub + 1) * q), (ng - 1 - j) * per_step + sub)

        @pl.when(j == ng - 1)
        def _():
            hbl_ref[0] = stb_ref[...]

    def forward_chunk(rows, c):
        dt, d_a, xs, b_mat = chunk_inputs(rows)
        row = lax.broadcasted_iota(jnp.int32, (q, q), 0)
        lane = lax.broadcasted_iota(jnp.int32, (q, q), 1)
        sums = _cumsums(d_a, tri_ref[...])
        acs = jnp.where((lane & (HEAD_DIRS - 1)) < bwd_lanes, sums[0:q], sums[q:2 * q])
        spread = _dot(_packed_operand(acs, dt), sel_ref[:, 0:SEL_FWD])
        half = (lax.broadcasted_iota(jnp.int32, (q, SSD_WIDTH), 1) & (LANES - 1)) < SSD_HEAD_DIM

        def per_channel(first_block):
            even = [spread[:, (first_block + 2 * i) * LANES:(first_block + 2 * i + 1) * LANES]
                    for i in range(SSD_HEADS // 2)]
            odd = [spread[:, (first_block + 2 * i + 1) * LANES:(first_block + 2 * i + 2) * LANES]
                   for i in range(SSD_HEADS // 2)]
            return jnp.where(half, jnp.concatenate(even, axis=1), jnp.concatenate(odd, axis=1))

        sum_f = per_channel(fwd_lanes)
        e_f = jnp.exp2(sum_f)
        st_f = stf_ref[...].astype(BF16) if emit_y else None
        w_full = spread[:, SEL_SPREAD:SEL_FWD] * jnp.exp2(sum_f[q - 1:q, :] - sum_f)
        _state_update(stf_ref, b_mat, (xs * w_full).astype(BF16), e_f[q - 1:q, :])
        if not emit_y:
            return

        acs_t = acs.T
        dt_t = dt.T
        src_t = acs_t - jnp.log2(dt_t)
        both_t = jnp.log2(dt_t[fwd_lanes:fwd_lanes + SSD_HEADS] + dt_t[bwd_lanes:bwd_lanes + SSD_HEADS])
        c_mat = xbc_ref[0, rows, SSD_WIDTH + SSD_GN:SSD_XBC]
        below = row > lane
        diag = row == lane

        cb = []
        for g in range(SSD_GROUPS):
            gs = slice(g * SSD_STATE, (g + 1) * SSD_STATE)
            cb.append(lax.dot_general(c_mat[:, gs], b_mat[:, gs], (((1,), (1,)), ((), ())),
                                      preferred_element_type=F32))

        y_parts = []
        for i in range(SSD_HEADS // 2):
            mats = []
            for h in (2 * i, 2 * i + 1):
                hf, hb = fwd_lanes + h, bwd_lanes + h
                tgt = jnp.where(below, spread[:, hf * LANES:(hf + 1) * LANES],
                                spread[:, hb * LANES:(hb + 1) * LANES])
                src = jnp.where(below, src_t[hf:hf + 1, :], src_t[hb:hb + 1, :])
                arg = jnp.where(diag, both_t[h:h + 1, :], tgt - src)
                mats.append((cb[h // (SSD_HEADS // SSD_GROUPS)] * jnp.exp2(arg)).astype(BF16))
            pair = xs[:, i * LANES:(i + 1) * LANES]
            top = jnp.where(lane < SSD_HEAD_DIM, pair, 0.0).astype(BF16)
            bot = jnp.where(lane >= SSD_HEAD_DIM, pair, 0.0).astype(BF16)
            y_parts.append(_dot(jnp.concatenate(mats, axis=1), jnp.concatenate([top, bot], axis=0)))
        y = jnp.concatenate(y_parts, axis=1)

        e_b = jnp.exp2(per_channel(bwd_lanes))
        st_b = hbe_ref[c]
        off_f, off_b = [], []
        for g in range(SSD_GROUPS):
            gs = slice(g * SSD_STATE, (g + 1) * SSD_STATE)
            cols = slice(g * HEAD_GROUP_COLS, (g + 1) * HEAD_GROUP_COLS)
            off_f.append(_dot(c_mat[:, gs], st_f[:, cols]))
            off_b.append(_dot(c_mat[:, gs], st_b[:, cols]))
        y = y + e_f * jnp.concatenate(off_f, axis=1) + e_b * jnp.concatenate(off_b, axis=1)
        y = y + dsk_ref[...] * xs

        z = z_ref[0, rows, :].astype(F32)
        y_ref[0, rows, :] = _norm_rows(y * _silu(z), g_ref[...]).astype(y_ref.dtype)

    @pl.when(j >= ng)
    def _():
        for sub in range(per_step):
            forward_chunk(slice(sub * q, (sub + 1) * q), (j - ng) * per_step + sub)

        @pl.when(j == 2 * ng - 1)
        def _():
            hfl_ref[0] = stf_ref[...]


def ssd(xbc, dt_raw, p, h0_f, h0_b, dt_bias, a_log, d_skip, norm_g, emit_y):
    nseq, t, _ = xbc.shape
    nc = t // SSD_CHUNK
    per_step = min(SSD_CHUNKS_PER_STEP, nc)
    ng = nc // per_step
    rows = per_step * SSD_CHUNK
    pad_row = lambda v: jnp.tile(v.reshape(1, HEAD_DIRS), (1, LANES // HEAD_DIRS))
    ones = np.ones((SSD_CHUNK, SSD_CHUNK), np.float32)
    tri = jnp.asarray(np.concatenate([np.tile(np.tril(ones), (1, 3)), np.tile(np.triu(ones), (1, 3))]), BF16)
    chunk = lambda s, j: (s, jnp.where(j < ng, ng - 1 - j, j - ng), 0)
    out_chunk = lambda s, j: (s, jnp.maximum(j - ng, 0), 0)
    const2 = lambda s, j: (0, 0)
    seq = lambda s, j: (s, 0, 0)
    state_shape = jax.ShapeDtypeStruct((nseq, SSD_STATE, SSD_WIDTH), F32)
    return pl.pallas_call(
        functools.partial(_ssd_kernel, ng=ng, per_step=per_step, emit_y=emit_y),
        grid=(nseq, 2 * ng),
        in_specs=[
            pl.BlockSpec((1, rows, SSD_XBC), chunk),
            pl.BlockSpec((1, rows, LANES), chunk),
            pl.BlockSpec((1, rows, SSD_WIDTH), out_chunk),
            pl.BlockSpec((1, SSD_STATE, SSD_WIDTH), seq),
            pl.BlockSpec((1, SSD_STATE, SSD_WIDTH), seq),
            pl.BlockSpec((1, LANES), const2),
            pl.BlockSpec((1, LANES), const2),
            pl.BlockSpec((1, SSD_WIDTH), const2),
            pl.BlockSpec((1, SSD_WIDTH), const2),
            pl.BlockSpec((2 * SSD_CHUNK, 3 * SSD_CHUNK), const2),
            pl.BlockSpec((SEL_ROWS, SEL_COLS), const2),
        ],
        out_specs=([pl.BlockSpec((1, rows, SSD_WIDTH), out_chunk)] if emit_y else [])
        + [pl.BlockSpec((1, SSD_STATE, SSD_WIDTH), seq)] * 2,
        out_shape=([jax.ShapeDtypeStruct((nseq, t, SSD_WIDTH), BF16)] if emit_y else []) + [state_shape] * 2,
        scratch_shapes=[
            pltpu.VMEM((nc, SSD_STATE, SSD_WIDTH), BF16),
            pltpu.VMEM((SSD_STATE, SSD_WIDTH), F32),
            pltpu.VMEM((SSD_STATE, SSD_WIDTH), F32),
        ],
        compiler_params=_params("arbitrary", "arbitrary"),
        name="ssd",
    )(xbc, dt_raw, p, h0_f, h0_b, pad_row(dt_bias), pad_row(a_log),
      jnp.repeat(d_skip, SSD_HEAD_DIM).reshape(1, SSD_WIDTH), norm_g.reshape(1, SSD_WIDTH), tri,
      _selection_matrix())


def _out_proj_kernel(y1_ref, y2_ref, w_ref, x_ref, gate_ref, g_ref, sh_ref, sc_ref,
                     xo_ref, h_ref, *, sub):
    y = jnp.concatenate([y1_ref[0], y2_ref[0]], axis=1)
    n = xo_ref.shape[-1]
    for n0 in range(0, n, sub):
        ns = slice(n0, n0 + sub)
        xo_ref[0, :, ns] = x_ref[0, :, ns] + gate_ref[0, :, ns] * _dot(y, w_ref[0, :, ns])
    hn = _norm_rows(xo_ref[0], g_ref[...])
    h_ref[0] = (hn * (1.0 + sc_ref[0]) + sh_ref[0]).astype(h_ref.dtype)


def out_proj(y_ssd, y_sc, w_out, layer, x, gate, g, shift, scale, tm=512, sub=1024):
    nb, length, d = x.shape
    k1, k2 = y_ssd.shape[-1], y_sc.shape[-1]
    tok = lambda b, i: (b, i, 0)
    per_b = lambda b, i: (b, 0, 0)
    return pl.pallas_call(
        functools.partial(_out_proj_kernel, sub=sub),
        grid=(nb, length // tm),
        in_specs=[
            pl.BlockSpec((1, tm, k1), tok),
            pl.BlockSpec((1, tm, k2), tok),
            pl.BlockSpec((1, k1 + k2, d), lambda b, i: (layer, 0, 0)),
            pl.BlockSpec((1, tm, d), tok),
            pl.BlockSpec((1, 1, d), per_b),
            pl.BlockSpec((1, d), lambda b, i: (0, 0)),
            pl.BlockSpec((1, 1, d), per_b),
            pl.BlockSpec((1, 1, d), per_b),
        ],
        out_specs=[pl.BlockSpec((1, tm, d), tok), pl.BlockSpec((1, tm, d), tok)],
        out_shape=[jax.ShapeDtypeStruct((nb, length, d), F32),
                   jax.ShapeDtypeStruct((nb, length, d), BF16)],
        compiler_params=_params("arbitrary", "arbitrary"),
        name="out_proj",
    )(y_ssd, y_sc, w_out, x, gate, g.reshape(1, d), shift, scale)


def _ffn_up_kernel(h_ref, wg_ref, wu_ref, o_ref, wgb_ref, wub_ref, *, sub_m):
    @pl.when((pl.program_id(1) == 0) & (pl.program_id(2) == 0))
    def _():
        wgb_ref[...] = wg_ref[0].astype(BF16)
        wub_ref[...] = wu_ref[0].astype(BF16)

    tm = h_ref.shape[1]
    for m0 in range(0, tm, sub_m):
        h = h_ref[0, m0:m0 + sub_m, :]
        o_ref[0, m0:m0 + sub_m, :] = (_silu(_dot(h, wgb_ref[...])) * _dot(h, wub_ref[...])).astype(o_ref.dtype)


def ffn_up(h, w_gate, w_up, layer, tm=2048, tn=512, sub_m=1024):
    nb, length, d = h.shape
    n = w_gate.shape[-1]
    tm = min(tm, length)
    w_spec = pl.BlockSpec((1, d, tn), lambda j, b, i: (layer, 0, j))
    return pl.pallas_call(
        functools.partial(_ffn_up_kernel, sub_m=sub_m),
        grid=(n // tn, nb, length // tm),
        in_specs=[pl.BlockSpec((1, tm, d), lambda j, b, i: (b, i, 0)), w_spec, w_spec],
        out_specs=pl.BlockSpec((1, tm, tn), lambda j, b, i: (b, i, j)),
        out_shape=jax.ShapeDtypeStruct((nb, length, n), BF16),
        scratch_shapes=[pltpu.VMEM((d, tn), BF16), pltpu.VMEM((d, tn), BF16)],
        compiler_params=_params("arbitrary", "arbitrary", "arbitrary"),
        name="ffn_up",
    )(h, w_gate, w_up)


def _ffn_down_kernel(a_ref, w_ref, x_ref, gate_ref, g_ref, sh_ref, sc_ref, *refs, sub_n):
    res_ref, h_ref = refs[0], refs[-1]
    n = res_ref.shape[2]
    a = a_ref[0]
    for n0 in range(0, n, sub_n):
        ns = slice(n0, n0 + sub_n)
        res_ref[0, :, ns] = x_ref[0, :, ns] + gate_ref[0, :, ns] * _dot(a, w_ref[0, :, ns])
    hn = _norm_rows(res_ref[0], g_ref[...])
    h_ref[0] = (hn * (1.0 + sc_ref[0]) + sh_ref[0]).astype(h_ref.dtype)


def ffn_down(act, w_down, layer, x, gate, g, shift, scale, emit_x, h_dtype, tm=512):
    nb, length, d = x.shape
    kk = act.shape[-1]
    tok = lambda b, i: (b, i, 0)
    per_b = lambda b, i: (b, 0, 0)
    if emit_x:
        out_shape = [jax.ShapeDtypeStruct((nb, length, d), F32), jax.ShapeDtypeStruct((nb, length, d), h_dtype)]
    else:
        assert h_dtype == F32
        out_shape = [jax.ShapeDtypeStruct((nb, length, d), F32)]
    return pl.pallas_call(
        functools.partial(_ffn_down_kernel, sub_n=d // 2),
        grid=(nb, length // tm),
        in_specs=[
            pl.BlockSpec((1, tm, kk), tok),
            pl.BlockSpec((1, kk, d), lambda b, i: (layer, 0, 0), pipeline_mode=pl.Buffered(1)),
            pl.BlockSpec((1, tm, d), tok),
            pl.BlockSpec((1, 1, d), per_b),
            pl.BlockSpec((1, d), lambda b, i: (0, 0)),
            pl.BlockSpec((1, 1, d), per_b),
            pl.BlockSpec((1, 1, d), per_b),
        ],
        out_specs=[pl.BlockSpec((1, tm, d), tok)] * len(out_shape),
        out_shape=out_shape,
        compiler_params=_params("arbitrary", "arbitrary", vmem=FFN_DOWN_VMEM_LIMIT),
        name="ffn_down",
    )(act, w_down, x, gate, g.reshape(1, d), shift, scale)


def _to_col_major(t, rows):
    b, length, ch = t.shape
    return t.reshape(b, rows, GRID_W, ch).transpose(0, 2, 1, 3).reshape(b, length, ch)


def _from_col_major(t, rows):
    b, length, ch = t.shape
    return t.reshape(b, GRID_W, rows, ch).transpose(0, 2, 1, 3).reshape(b, length, ch)


def _mixer(h_rows, nseq, w_ssd, w_dt, w_sc, l, layer, h0_f, h0_b):
    (conv_w, conv_b, dt_bias, a_log, d_skip, ssd_g, sc_w) = layer
    nb, length, _ = h_rows.shape
    t = nb * length // nseq
    tm = min(length, 1024)
    p, dt_raw = in_proj_ssd(h_rows, w_ssd, w_dt, l, tm=tm)
    p = p.reshape(nseq, t, p.shape[-1])
    dt_raw = dt_raw.reshape(nseq, t, LANES)
    xbc = conv_xbc(p, conv_w, conv_b)
    *y_ssd, h_f, h_b = ssd(xbc, dt_raw, p, h0_f, h0_b, dt_bias, a_log, d_skip, ssd_g, emit_y=w_sc is not None)
    y_ssd = y_ssd[0] if y_ssd else None
    y_sc = None
    if w_sc is not None:
        n_sc = w_sc.shape[-1]
        p_sc = matmul(h_rows, w_sc, l, BF16, tm=tm, tn=n_sc, sub=256)
        y_sc = conv_sc(p_sc.reshape(nseq, t, n_sc), sc_w)
    return y_ssd, y_sc, h_f, h_b


def kernel(x, c, ctx, c_ctx, ada_w, ada_b, mix_norm_g, w_in, ssd_conv_w, ssd_conv_b, ssd_dt_bias,
           ssd_a_log, ssd_d, ssd_norm_g, sc_conv_w, w_out, ffn_norm_g, w_gate, w_up, w_down,
           final_norm_g):
    batch, seq, d = x.shape
    depth = ada_w.shape[0]
    ctx_len = ctx.shape[1]
    rows = seq // GRID_W
    n_ctx_rows = batch * ctx_len

    cc = jnp.zeros((16, d), F32).at[:batch].set(c).at[batch].set(c_ctx)
    mods = adaln(cc, ada_w, ada_b)

    def mod_vectors(l):
        mx = [mods[l, :batch, i * d:(i + 1) * d].reshape(batch, 1, d) for i in range(6)]
        mc = [mods[l, batch:batch + 1, i * d:(i + 1) * d].reshape(1, 1, d) for i in range(6)]
        return mx, mc

    dt_lo = SSD_WIDTH + SSD_XBC
    w_ssd = w_in[:, :, :dt_lo].astype(BF16)
    w_sc = w_in[:, :, SSD_COLS:].astype(BF16)
    w_dt = jnp.tile(w_in[:, :, dt_lo:SSD_COLS].astype(BF16), (1, 1, LANES // HEAD_DIRS))
    w_out_b = w_out.astype(BF16)
    w_down_b = w_down.astype(BF16)

    h_ctx = ctx.reshape(1, n_ctx_rows, d)
    mx, mc = mod_vectors(0)
    hx = norm_mod(x, mix_norm_g[0], mx[0], mx[1])
    hc = norm_mod(h_ctx, mix_norm_g[0], mc[0], mc[1])
    zero_state = jnp.zeros((batch, SSD_STATE, SSD_WIDTH), F32)
    zeros_d = jnp.zeros((batch, 1, d), F32)
    out = None

    for l in range(depth):
        last = l == depth - 1
        layer = (ssd_conv_w[l], ssd_conv_b[l], ssd_dt_bias[l], ssd_a_log[l], ssd_d[l],
                 ssd_norm_g[l], sc_conv_w[l])
        if not last:
            mx_next, mc_next = mod_vectors(l + 1)

        if last:
            _, _, state_f, state_b = _mixer(hc, batch, w_ssd, w_dt, None, l, layer,
                                            zero_state, zero_state)
        else:
            y_ssd, y_sc, state_f, state_b = _mixer(hc, batch, w_ssd, w_dt, w_sc, l, layer,
                                                   zero_state, zero_state)
            h_ctx, hf = out_proj(y_ssd.reshape(1, n_ctx_rows, -1), y_sc.reshape(1, n_ctx_rows, -1),
                                 w_out_b, l, h_ctx, mc[2], ffn_norm_g[l], mc[3], mc[4])
            act = ffn_up(hf, w_gate, w_up, l)
            h_ctx, hc = ffn_down(act, w_down_b, l, h_ctx, mc[5], mix_norm_g[l + 1],
                                 mc_next[0], mc_next[1], emit_x=True, h_dtype=BF16)

        col_major = l % 2 == 1
        if col_major:
            hx = _to_col_major(hx, rows)
        y_ssd, y_sc, _, _ = _mixer(hx, batch, w_ssd, w_dt, w_sc, l, layer, state_f, state_b)
        if col_major:
            y_ssd = _from_col_major(y_ssd, rows)
            y_sc = _from_col_major(y_sc, rows)
        x, hf = out_proj(y_ssd, y_sc, w_out_b, l, x, mx[2], ffn_norm_g[l], mx[3], mx[4])
        act = ffn_up(hf, w_gate, w_up, l)
        if last:
            (out,) = ffn_down(act, w_down_b, l, x, mx[5], final_norm_g, zeros_d, zeros_d,
                              emit_x=False, h_dtype=x.dtype)
        else:
            x, hx = ffn_down(act, w_down_b, l, x, mx[5], mix_norm_g[l + 1],
                             mx_next[0], mx_next[1], emit_x=True, h_dtype=BF16)
            mx, mc = mx_next, mc_next
    return out
```

```python
import functools

import jax
import jax.numpy as jnp
import numpy as np
from jax import lax
from jax.experimental import pallas as pl
from jax.experimental.pallas import tpu as pltpu

F32 = jnp.float32
BF16 = jnp.bfloat16

D_MODEL = 2048
GRID_W = 64
NORM_EPS = 1e-6
SSD_WIDTH = 1024
SSD_HEAD_DIM = 64
SSD_HEADS = 16
SSD_GROUPS = 2
SSD_STATE = 128
SSD_CONV_W = 5
SSD_CHUNK = 128
SSD_CHUNKS_PER_STEP = 8
SSD_GN = SSD_GROUPS * SSD_STATE
SSD_XBC = SSD_WIDTH + 2 * SSD_GN
SSD_COLS = SSD_WIDTH + SSD_XBC + 2 * SSD_HEADS
SC_WIDTH = 1024
SC_CONV_W = 3
FFN_HIDDEN = 5632
LANES = 128
HEAD_GROUP_COLS = SSD_WIDTH // SSD_GROUPS
VMEM_LIMIT = 56 * 1024 * 1024
FFN_DOWN_VMEM_LIMIT = 62 * 1024 * 1024
LOG2_E = 1.4426950408889634


def _params(*sem, vmem=VMEM_LIMIT):
    return pltpu.CompilerParams(dimension_semantics=sem, vmem_limit_bytes=vmem)


def _silu(v):
    return v * jax.nn.sigmoid(v)


def _softplus(v):
    return jnp.maximum(v, 0.0) + jnp.log1p(jnp.exp(-jnp.abs(v)))


def _norm_rows(v, g):
    ms = jnp.mean(v * v, axis=-1, keepdims=True)
    return v * lax.rsqrt(ms + NORM_EPS) * g


def _dot(a, b):
    return jnp.dot(a, b, preferred_element_type=F32)


def _adaln_kernel(c_ref, w_ref, b_ref, o_ref):
    s = _silu(c_ref[...]).astype(BF16)
    o_ref[0] = _dot(s, w_ref[0].astype(BF16)) + b_ref[0]


def adaln(cc, ada_w, ada_b, tn=1536):
    depth, d, n = ada_w.shape
    rows = cc.shape[0]
    return pl.pallas_call(
        _adaln_kernel,
        grid=(depth, n // tn),
        in_specs=[
            pl.BlockSpec((rows, d), lambda l, j: (0, 0)),
            pl.BlockSpec((1, d, tn), lambda l, j: (l, 0, j)),
            pl.BlockSpec((1, 1, tn), lambda l, j: (l, 0, j)),
        ],
        out_specs=pl.BlockSpec((1, rows, tn), lambda l, j: (l, 0, j)),
        out_shape=jax.ShapeDtypeStruct((depth, rows, n), F32),
        compiler_params=_params("arbitrary", "arbitrary"),
        name="adaln",
    )(cc, ada_w, ada_b.reshape(depth, 1, n))


def _norm_mod_kernel(x_ref, g_ref, sh_ref, sc_ref, o_ref):
    y = _norm_rows(x_ref[0], g_ref[...])
    o_ref[0] = (y * (1.0 + sc_ref[0]) + sh_ref[0]).astype(o_ref.dtype)


def norm_mod(x, g, shift, scale, tr=1024):
    nb, length, d = x.shape
    return pl.pallas_call(
        _norm_mod_kernel,
        grid=(nb, length // tr),
        in_specs=[
            pl.BlockSpec((1, tr, d), lambda b, i: (b, i, 0)),
            pl.BlockSpec((1, d), lambda b, i: (0, 0)),
            pl.BlockSpec((1, 1, d), lambda b, i: (b, 0, 0)),
            pl.BlockSpec((1, 1, d), lambda b, i: (b, 0, 0)),
        ],
        out_specs=pl.BlockSpec((1, tr, d), lambda b, i: (b, i, 0)),
        out_shape=jax.ShapeDtypeStruct((nb, length, d), BF16),
        compiler_params=_params("arbitrary", "arbitrary"),
        name="norm_mod",
    )(x, g.reshape(1, d), shift, scale)


def _matmul_kernel(a_ref, w_ref, o_ref, *, sub):
    a = a_ref[0]
    tn = o_ref.shape[-1]
    for n0 in range(0, tn, sub):
        o_ref[0, :, n0:n0 + sub] = _dot(a, w_ref[0, :, n0:n0 + sub]).astype(o_ref.dtype)


def matmul(a, w, layer, out_dtype, tm, tn, sub):
    nb, length, k = a.shape
    n = w.shape[-1]
    return pl.pallas_call(
        functools.partial(_matmul_kernel, sub=sub),
        grid=(n // tn, nb, length // tm),
        in_specs=[
            pl.BlockSpec((1, tm, k), lambda j, b, i: (b, i, 0)),
            pl.BlockSpec((1, k, tn), lambda j, b, i: (layer, 0, j)),
        ],
        out_specs=pl.BlockSpec((1, tm, tn), lambda j, b, i: (b, i, j)),
        out_shape=jax.ShapeDtypeStruct((nb, length, n), out_dtype),
        compiler_params=_params("arbitrary", "arbitrary", "arbitrary"),
        name="matmul",
    )(a, w)


def _in_proj_ssd_kernel(a_ref, w_ref, wdt_ref, o_ref, dt_ref, *, sub):
    a = a_ref[0]
    for n0 in range(0, o_ref.shape[-1], sub):
        o_ref[0, :, n0:n0 + sub] = _dot(a, w_ref[0, :, n0:n0 + sub]).astype(o_ref.dtype)
    dt_ref[0] = _dot(a, wdt_ref[0])


def in_proj_ssd(a, w, w_dt, layer, tm, sub=256):
    nb, length, k = a.shape
    n = w.shape[-1]
    tok = lambda b, i: (b, i, 0)
    of_layer = lambda b, i: (layer, 0, 0)
    return pl.pallas_call(
        functools.partial(_in_proj_ssd_kernel, sub=sub),
        grid=(nb, length // tm),
        in_specs=[pl.BlockSpec((1, tm, k), tok), pl.BlockSpec((1, k, n), of_layer),
                  pl.BlockSpec((1, k, LANES), of_layer)],
        out_specs=[pl.BlockSpec((1, tm, n), tok), pl.BlockSpec((1, tm, LANES), tok)],
        out_shape=[jax.ShapeDtypeStruct((nb, length, n), BF16), jax.ShapeDtypeStruct((nb, length, LANES), F32)],
        compiler_params=_params("arbitrary", "arbitrary"),
        name="in_proj_ssd",
    )(a, w, w_dt)


def _shifted_taps(pad_ref, r0, rows, taps):
    half = taps // 2
    win = pad_ref[r0:r0 + rows + 32, :]
    out = []
    for k in range(taps):
        shift = (half - k) % (rows + 32)
        rolled = win if shift == 0 else pltpu.roll(win, shift, axis=0)
        out.append(rolled[16:16 + rows, :])
    return out


def _fill_padded(pad_ref, vals):
    t = vals.shape[0]
    zeros = jnp.zeros((16, vals.shape[1]), F32)
    pad_ref[0:16, :] = zeros
    pad_ref[t + 16:t + 32, :] = zeros
    pad_ref[16:t + 16, :] = vals


def _conv_xbc_kernel(x_ref, w_ref, b_ref, o_ref, pad_ref, *, rows):
    t = x_ref.shape[1]
    _fill_padded(pad_ref, x_ref[0].astype(F32))
    for r0 in range(0, t, rows):
        taps = _shifted_taps(pad_ref, r0, rows, SSD_CONV_W)
        acc = b_ref[...] + w_ref[0:1, :] * taps[0]
        for k in range(1, SSD_CONV_W):
            acc = acc + w_ref[k:k + 1, :] * taps[k]
        o_ref[0, r0:r0 + rows, :] = _silu(acc).astype(o_ref.dtype)


def conv_xbc(p, conv_w, conv_b, ct=256):
    nseq, t, _ = p.shape
    off = SSD_WIDTH // ct
    w8 = jnp.zeros((8, SSD_XBC), F32).at[:SSD_CONV_W].set(conv_w)
    return pl.pallas_call(
        functools.partial(_conv_xbc_kernel, rows=min(t, 256)),
        grid=(nseq, SSD_XBC // ct),
        in_specs=[
            pl.BlockSpec((1, t, ct), lambda s, j: (s, 0, off + j)),
            pl.BlockSpec((8, ct), lambda s, j: (0, j)),
            pl.BlockSpec((1, ct), lambda s, j: (0, j)),
        ],
        out_specs=pl.BlockSpec((1, t, ct), lambda s, j: (s, 0, j)),
        out_shape=jax.ShapeDtypeStruct((nseq, t, SSD_XBC), BF16),
        scratch_shapes=[pltpu.VMEM((t + 32, ct), F32)],
        compiler_params=_params("arbitrary", "arbitrary"),
        name="conv_xbc",
    )(p, w8, conv_b.reshape(1, SSD_XBC))


def _conv_sc_kernel(gb_ref, gc_ref, v_ref, w_ref, o_ref, pad_ref, *, rows):
    t = v_ref.shape[1]
    _fill_padded(pad_ref, gc_ref[0].astype(F32) * v_ref[0].astype(F32))
    for r0 in range(0, t, rows):
        taps = _shifted_taps(pad_ref, r0, rows, SC_CONV_W)
        acc = w_ref[0:1, :] * taps[0]
        for k in range(1, SC_CONV_W):
            acc = acc + w_ref[k:k + 1, :] * taps[k]
        gate = gb_ref[0, r0:r0 + rows, :].astype(F32)
        o_ref[0, r0:r0 + rows, :] = (gate * acc).astype(o_ref.dtype)


def conv_sc(p, sc_w, ct=512):
    nseq, t, _ = p.shape
    base = 0
    step = SC_WIDTH // ct
    w8 = jnp.zeros((8, SC_WIDTH), F32).at[:SC_CONV_W].set(sc_w)
    col = lambda k: pl.BlockSpec((1, t, ct), lambda s, j: (s, 0, base + k * step + j))
    return pl.pallas_call(
        functools.partial(_conv_sc_kernel, rows=min(t, 256)),
        grid=(nseq, SC_WIDTH // ct),
        in_specs=[col(0), col(1), col(2), pl.BlockSpec((8, ct), lambda s, j: (0, j))],
        out_specs=pl.BlockSpec((1, t, ct), lambda s, j: (s, 0, j)),
        out_shape=jax.ShapeDtypeStruct((nseq, t, SC_WIDTH), BF16),
        scratch_shapes=[pltpu.VMEM((t + 32, ct), F32)],
        compiler_params=_params("arbitrary", "arbitrary"),
        name="conv_sc",
    )(p, p, p, w8)


def _split3(v):
    hi = v.astype(BF16)
    r1 = v - hi.astype(F32)
    mid = r1.astype(BF16)
    return hi, mid, (r1 - mid.astype(F32)).astype(BF16)


def _cumsums(v, tri):
    return _dot(tri, jnp.concatenate(_split3(v), axis=0))


HEAD_DIRS = 2 * SSD_HEADS
SEL_ROWS = 2 * LANES
SEL_SPREAD = HEAD_DIRS * LANES
SEL_FWD = SEL_SPREAD + SSD_WIDTH
SEL_COLS = SEL_FWD + 2 * SSD_WIDTH


def _selection_matrix():
    s = np.zeros((SEL_ROWS, SEL_COLS), np.float32)
    for band in range(3):
        base = band * 2 * HEAD_DIRS
        for k in range(HEAD_DIRS):
            s[base + k, k * LANES:(k + 1) * LANES] = 1.0
        for h in range(SSD_HEADS):
            ch = slice(h * SSD_HEAD_DIM, (h + 1) * SSD_HEAD_DIM)
            s[base + HEAD_DIRS + h, SEL_SPREAD:SEL_FWD][ch] = 1.0
            s[base + SSD_HEADS + h, SEL_FWD:SEL_FWD + SSD_WIDTH][ch] = 1.0
            s[base + HEAD_DIRS + SSD_HEADS + h, SEL_FWD + SSD_WIDTH:SEL_COLS][ch] = 1.0
    return jnp.asarray(s, BF16)


def _packed_operand(acs, dt):
    lane = lax.broadcasted_iota(jnp.int32, acs.shape, 1)
    packed = jnp.where((lane & (2 * HEAD_DIRS - 1)) < HEAD_DIRS, acs, dt)
    hi, mid, lo = _split3(packed)
    first = jnp.where(lane < 2 * HEAD_DIRS, hi.astype(F32), mid.astype(F32)).astype(BF16)
    return jnp.concatenate([first, lo], axis=1)


def _state_update(st_ref, b_mat, xw, decay_row):
    for g in range(SSD_GROUPS):
        cols = slice(g * HEAD_GROUP_COLS, (g + 1) * HEAD_GROUP_COLS)
        contrib = lax.dot_general(
            b_mat[:, g * SSD_STATE:(g + 1) * SSD_STATE], xw[:, cols],
            (((0,), (0,)), ((), ())), preferred_element_type=F32)
        st_ref[:, cols] = st_ref[:, cols] * decay_row[:, cols] + contrib


def _ssd_kernel(xbc_ref, dt_ref, z_ref, h0f_ref, h0b_ref, dtb_ref, alog_ref, dsk_ref, g_ref, tri_ref,
                sel_ref, *refs, ng, per_step, emit_y):
    y_ref = refs[0] if emit_y else None
    hfl_ref, hbl_ref, hbe_ref, stf_ref, stb_ref = refs[-5:]
    q = SSD_CHUNK
    j = pl.program_id(1)
    fwd_lanes = 0
    bwd_lanes = SSD_HEADS

    def chunk_inputs(rows):
        dt = _softplus(dt_ref[0, rows, :] + dtb_ref[...])
        d_a = dt * (-jnp.exp(alog_ref[...]) * LOG2_E)
        xs = xbc_ref[0, rows, 0:SSD_WIDTH].astype(F32)
        b_mat = xbc_ref[0, rows, SSD_WIDTH:SSD_WIDTH + SSD_GN]
        return dt, d_a, xs, b_mat

    @pl.when(j == 0)
    def _():
        stf_ref[...] = h0f_ref[0]
        stb_ref[...] = h0b_ref[0]

    def backward_chunk(rows, c):
        dt, d_a, xs, b_mat = chunk_inputs(rows)
        acs_b = _cumsums(d_a, tri_ref[q:2 * q, :])
        spread = _dot(_packed_operand(acs_b, dt), sel_ref[:, SEL_FWD:SEL_COLS])
        sum_b = spread[:, 0:SSD_WIDTH]
        tot = sum_b[0:1, :]
        w_full = spread[:, SSD_WIDTH:2 * SSD_WIDTH] * jnp.exp2(tot - sum_b)
        xw = (xs * w_full).astype(BF16)
        if emit_y:
            hbe_ref[c] = stb_ref[...].astype(BF16)
        _state_update(stb_ref, b_mat, xw, jnp.exp2(tot))

    @pl.when(j < ng)
    def _():
        for sub in reversed(range(per_step)):
            backward_chunk(slice(sub * q, (sub + 1) * q), (ng - 1 - j) * per_step + sub)

        @pl.when(j == ng - 1)
        def _():
            hbl_ref[0] = stb_ref[...]

    def forward_chunk(rows, c):
        dt, d_a, xs, b_mat = chunk_inputs(rows)
        row = lax.broadcasted_iota(jnp.int32, (q, q), 0)
        lane = lax.broadcasted_iota(jnp.int32, (q, q), 1)
        sums = _cumsums(d_a, tri_ref[...])
        acs = jnp.where((lane & (HEAD_DIRS - 1)) < bwd_lanes, sums[0:q], sums[q:2 * q])
        spread = _dot(_packed_operand(acs, dt), sel_ref[:, 0:SEL_FWD])
        half = (lax.broadcasted_iota(jnp.int32, (q, SSD_WIDTH), 1) & (LANES - 1)) < SSD_HEAD_DIM

        def per_channel(first_block):
            even = [spread[:, (first_block + 2 * i) * LANES:(first_block + 2 * i + 1) * LANES]
                    for i in range(SSD_HEADS // 2)]
            odd = [spread[:, (first_block + 2 * i + 1) * LANES:(first_block + 2 * i + 2) * LANES]
                   for i in range(SSD_HEADS // 2)]
            return jnp.where(half, jnp.concatenate(even, axis=1), jnp.concatenate(odd, axis=1))

        sum_f = per_channel(fwd_lanes)
        e_f = jnp.exp2(sum_f)

        def leave_chunk():
            w_full = spread[:, SEL_SPREAD:SEL_FWD] * jnp.exp2(sum_f[q - 1:q, :] - sum_f)
            _state_update(stf_ref, b_mat, (xs * w_full).astype(BF16), e_f[q - 1:q, :])

        if not emit_y:
            leave_chunk()
            return

        acs_t = acs.T
        dt_t = dt.T
        src_t = acs_t - jnp.log2(dt_t)
        both_t = jnp.log2(dt_t[fwd_lanes:fwd_lanes + SSD_HEADS] + dt_t[bwd_lanes:bwd_lanes + SSD_HEADS])
        c_mat = xbc_ref[0, rows, SSD_WIDTH + SSD_GN:SSD_XBC]
        below = row > lane
        diag = row == lane

        cb = []
        for g in range(SSD_GROUPS):
            gs = slice(g * SSD_STATE, (g + 1) * SSD_STATE)
            cb.append(lax.dot_general(c_mat[:, gs], b_mat[:, gs], (((1,), (1,)), ((), ())),
                                      preferred_element_type=F32))

        y_parts = []
        for i in range(SSD_HEADS // 2):
            mats = []
            for h in (2 * i, 2 * i + 1):
                hf, hb = fwd_lanes + h, bwd_lanes + h
                tgt = jnp.where(below, spread[:, hf * LANES:(hf + 1) * LANES],
                                spread[:, hb * LANES:(hb + 1) * LANES])
                src = jnp.where(below, src_t[hf:hf + 1, :], src_t[hb:hb + 1, :])
                arg = jnp.where(diag, both_t[h:h + 1, :], tgt - src)
                mats.append((cb[h // (SSD_HEADS // SSD_GROUPS)] * jnp.exp2(arg)).astype(BF16))
            pair = xs[:, i * LANES:(i + 1) * LANES]
            top = jnp.where(lane < SSD_HEAD_DIM, pair, 0.0).astype(BF16)
            bot = jnp.where(lane >= SSD_HEAD_DIM, pair, 0.0).astype(BF16)
            y_parts.append(_dot(jnp.concatenate(mats, axis=1), jnp.concatenate([top, bot], axis=0)))
        y = jnp.concatenate(y_parts, axis=1)

        e_b = jnp.exp2(per_channel(bwd_lanes))
        st_f = stf_ref[...].astype(BF16)
        st_b = hbe_ref[c]
        off_f, off_b = [], []
        for g in range(SSD_GROUPS):
            gs = slice(g * SSD_STATE, (g + 1) * SSD_STATE)
            cols = slice(g * HEAD_GROUP_COLS, (g + 1) * HEAD_GROUP_COLS)
            off_f.append(_dot(c_mat[:, gs], st_f[:, cols]))
            off_b.append(_dot(c_mat[:, gs], st_b[:, cols]))
        y = y + e_f * jnp.concatenate(off_f, axis=1) + e_b * jnp.concatenate(off_b, axis=1)
        y = y + dsk_ref[...] * xs
        leave_chunk()

        z = z_ref[0, rows, :].astype(F32)
        y_ref[0, rows, :] = _norm_rows(y * _silu(z), g_ref[...]).astype(y_ref.dtype)

    @pl.when(j >= ng)
    def _():
        for sub in range(per_step):
            forward_chunk(slice(sub * q, (sub + 1) * q), (j - ng) * per_step + sub)

        @pl.when(j == 2 * ng - 1)
        def _():
            hfl_ref[0] = stf_ref[...]


def ssd(xbc, dt_raw, p, h0_f, h0_b, dt_bias, a_log, d_skip, norm_g, emit_y):
    nseq, t, _ = xbc.shape
    nc = t // SSD_CHUNK
    per_step = min(SSD_CHUNKS_PER_STEP, nc)
    ng = nc // per_step
    rows = per_step * SSD_CHUNK
    pad_row = lambda v: jnp.tile(v.reshape(1, HEAD_DIRS), (1, LANES // HEAD_DIRS))
    ones = np.ones((SSD_CHUNK, SSD_CHUNK), np.float32)
    tri = jnp.asarray(np.concatenate([np.tile(np.tril(ones), (1, 3)), np.tile(np.triu(ones), (1, 3))]), BF16)
    chunk = lambda s, j: (s, jnp.where(j < ng, ng - 1 - j, j - ng), 0)
    out_chunk = lambda s, j: (s, jnp.maximum(j - ng, 0), 0)
    const2 = lambda s, j: (0, 0)
    seq = lambda s, j: (s, 0, 0)
    state_shape = jax.ShapeDtypeStruct((nseq, SSD_STATE, SSD_WIDTH), F32)
    return pl.pallas_call(
        functools.partial(_ssd_kernel, ng=ng, per_step=per_step, emit_y=emit_y),
        grid=(nseq, 2 * ng),
        in_specs=[
            pl.BlockSpec((1, rows, SSD_XBC), chunk),
            pl.BlockSpec((1, rows, LANES), chunk),
            pl.BlockSpec((1, rows, SSD_WIDTH), out_chunk),
            pl.BlockSpec((1, SSD_STATE, SSD_WIDTH), seq),
            pl.BlockSpec((1, SSD_STATE, SSD_WIDTH), seq),
            pl.BlockSpec((1, LANES), const2),
            pl.BlockSpec((1, LANES), const2),
            pl.BlockSpec((1, SSD_WIDTH), const2),
            pl.BlockSpec((1, SSD_WIDTH), const2),
            pl.BlockSpec((2 * SSD_CHUNK, 3 * SSD_CHUNK), const2),
            pl.BlockSpec((SEL_ROWS, SEL_COLS), const2),
        ],
        out_specs=([pl.BlockSpec((1, rows, SSD_WIDTH), out_chunk)] if emit_y else [])
        + [pl.BlockSpec((1, SSD_STATE, SSD_WIDTH), seq)] * 2,
        out_shape=([jax.ShapeDtypeStruct((nseq, t, SSD_WIDTH), BF16)] if emit_y else []) + [state_shape] * 2,
        scratch_shapes=[
            pltpu.VMEM((nc, SSD_STATE, SSD_WIDTH), BF16),
            pltpu.VMEM((SSD_STATE, SSD_WIDTH), F32),
            pltpu.VMEM((SSD_STATE, SSD_WIDTH), F32),
        ],
        compiler_params=_params("arbitrary", "arbitrary"),
        name="ssd",
    )(xbc, dt_raw, p, h0_f, h0_b, pad_row(dt_bias), pad_row(a_log),
      jnp.repeat(d_skip, SSD_HEAD_DIM).reshape(1, SSD_WIDTH), norm_g.reshape(1, SSD_WIDTH), tri,
      _selection_matrix())


def _out_proj_kernel(y1_ref, y2_ref, w_ref, x_ref, gate_ref, g_ref, sh_ref, sc_ref,
                     xo_ref, h_ref, *, sub):
    y = jnp.concatenate([y1_ref[0], y2_ref[0]], axis=1)
    n = xo_ref.shape[-1]
    for n0 in range(0, n, sub):
        ns = slice(n0, n0 + sub)
        xo_ref[0, :, ns] = x_ref[0, :, ns] + gate_ref[0, :, ns] * _dot(y, w_ref[0, :, ns])
    hn = _norm_rows(xo_ref[0], g_ref[...])
    h_ref[0] = (hn * (1.0 + sc_ref[0]) + sh_ref[0]).astype(h_ref.dtype)


def out_proj(y_ssd, y_sc, w_out, layer, x, gate, g, shift, scale, tm=512, sub=1024):
    nb, length, d = x.shape
    k1, k2 = y_ssd.shape[-1], y_sc.shape[-1]
    tok = lambda b, i: (b, i, 0)
    per_b = lambda b, i: (b, 0, 0)
    return pl.pallas_call(
        functools.partial(_out_proj_kernel, sub=sub),
        grid=(nb, length // tm),
        in_specs=[
            pl.BlockSpec((1, tm, k1), tok),
            pl.BlockSpec((1, tm, k2), tok),
            pl.BlockSpec((1, k1 + k2, d), lambda b, i: (layer, 0, 0)),
            pl.BlockSpec((1, tm, d), tok),
            pl.BlockSpec((1, 1, d), per_b),
            pl.BlockSpec((1, d), lambda b, i: (0, 0)),
            pl.BlockSpec((1, 1, d), per_b),
            pl.BlockSpec((1, 1, d), per_b),
        ],
        out_specs=[pl.BlockSpec((1, tm, d), tok), pl.BlockSpec((1, tm, d), tok)],
        out_shape=[jax.ShapeDtypeStruct((nb, length, d), F32),
                   jax.ShapeDtypeStruct((nb, length, d), BF16)],
        compiler_params=_params("arbitrary", "arbitrary"),
        name="out_proj",
    )(y_ssd, y_sc, w_out, x, gate, g.reshape(1, d), shift, scale)


def _ffn_up_kernel(h_ref, wg_ref, wu_ref, o_ref, wgb_ref, wub_ref, *, sub_m):
    @pl.when((pl.program_id(1) == 0) & (pl.program_id(2) == 0))
    def _():
        wgb_ref[...] = wg_ref[0].astype(BF16)
        wub_ref[...] = wu_ref[0].astype(BF16)

    tm = h_ref.shape[1]
    for m0 in range(0, tm, sub_m):
        h = h_ref[0, m0:m0 + sub_m, :]
        o_ref[0, m0:m0 + sub_m, :] = (_silu(_dot(h, wgb_ref[...])) * _dot(h, wub_ref[...])).astype(o_ref.dtype)


def ffn_up(h, w_gate, w_up, layer, tm=2048, tn=512, sub_m=1024):
    nb, length, d = h.shape
    n = w_gate.shape[-1]
    tm = min(tm, length)
    w_spec = pl.BlockSpec((1, d, tn), lambda j, b, i: (layer, 0, j))
    return pl.pallas_call(
        functools.partial(_ffn_up_kernel, sub_m=sub_m),
        grid=(n // tn, nb, length // tm),
        in_specs=[pl.BlockSpec((1, tm, d), lambda j, b, i: (b, i, 0)), w_spec, w_spec],
        out_specs=pl.BlockSpec((1, tm, tn), lambda j, b, i: (b, i, j)),
        out_shape=jax.ShapeDtypeStruct((nb, length, n), BF16),
        scratch_shapes=[pltpu.VMEM((d, tn), BF16), pltpu.VMEM((d, tn), BF16)],
        compiler_params=_params("arbitrary", "arbitrary", "arbitrary"),
        name="ffn_up",
    )(h, w_gate, w_up)


def _ffn_down_kernel(a_ref, w_ref, x_ref, gate_ref, g_ref, sh_ref, sc_ref, *refs, sub_n):
    res_ref, h_ref = refs[0], refs[-1]
    n = res_ref.shape[2]
    a = a_ref[0]
    for n0 in range(0, n, sub_n):
        ns = slice(n0, n0 + sub_n)
        res_ref[0, :, ns] = x_ref[0, :, ns] + gate_ref[0, :, ns] * _dot(a, w_ref[0, :, ns])
    hn = _norm_rows(res_ref[0], g_ref[...])
    h_ref[0] = (hn * (1.0 + sc_ref[0]) + sh_ref[0]).astype(h_ref.dtype)


def ffn_down(act, w_down, layer, x, gate, g, shift, scale, emit_x, h_dtype, tm=512):
    nb, length, d = x.shape
    kk = act.shape[-1]
    tok = lambda b, i: (b, i, 0)
    per_b = lambda b, i: (b, 0, 0)
    if emit_x:
        out_shape = [jax.ShapeDtypeStruct((nb, length, d), F32), jax.ShapeDtypeStruct((nb, length, d), h_dtype)]
    else:
        assert h_dtype == F32
        out_shape = [jax.ShapeDtypeStruct((nb, length, d), F32)]
    return pl.pallas_call(
        functools.partial(_ffn_down_kernel, sub_n=d // 2),
        grid=(nb, length // tm),
        in_specs=[
            pl.BlockSpec((1, tm, kk), tok),
            pl.BlockSpec((1, kk, d), lambda b, i: (layer, 0, 0), pipeline_mode=pl.Buffered(1)),
            pl.BlockSpec((1, tm, d), tok),
            pl.BlockSpec((1, 1, d), per_b),
            pl.BlockSpec((1, d), lambda b, i: (0, 0)),
            pl.BlockSpec((1, 1, d), per_b),
            pl.BlockSpec((1, 1, d), per_b),
        ],
        out_specs=[pl.BlockSpec((1, tm, d), tok)] * len(out_shape),
        out_shape=out_shape,
        compiler_params=_params("arbitrary", "arbitrary", vmem=FFN_DOWN_VMEM_LIMIT),
        name="ffn_down",
    )(act, w_down, x, gate, g.reshape(1, d), shift, scale)


def _to_col_major(t, rows):
    b, length, ch = t.shape
    return t.reshape(b, rows, GRID_W, ch).transpose(0, 2, 1, 3).reshape(b, length, ch)


def _from_col_major(t, rows):
    b, length, ch = t.shape
    return t.reshape(b, GRID_W, rows, ch).transpose(0, 2, 1, 3).reshape(b, length, ch)


def _mixer(h_rows, nseq, w_ssd, w_dt, w_sc, l, layer, h0_f, h0_b):
    (conv_w, conv_b, dt_bias, a_log, d_skip, ssd_g, sc_w) = layer
    nb, length, _ = h_rows.shape
    t = nb * length // nseq
    tm = min(length, 1024)
    p, dt_raw = in_proj_ssd(h_rows, w_ssd, w_dt, l, tm=tm)
    p = p.reshape(nseq, t, p.shape[-1])
    dt_raw = dt_raw.reshape(nseq, t, LANES)
    xbc = conv_xbc(p, conv_w, conv_b)
    *y_ssd, h_f, h_b = ssd(xbc, dt_raw, p, h0_f, h0_b, dt_bias, a_log, d_skip, ssd_g, emit_y=w_sc is not None)
    y_ssd = y_ssd[0] if y_ssd else None
    y_sc = None
    if w_sc is not None:
        n_sc = w_sc.shape[-1]
        p_sc = matmul(h_rows, w_sc, l, BF16, tm=tm, tn=n_sc, sub=256)
        y_sc = conv_sc(p_sc.reshape(nseq, t, n_sc), sc_w)
    return y_ssd, y_sc, h_f, h_b


def kernel(x, c, ctx, c_ctx, ada_w, ada_b, mix_norm_g, w_in, ssd_conv_w, ssd_conv_b, ssd_dt_bias,
           ssd_a_log, ssd_d, ssd_norm_g, sc_conv_w, w_out, ffn_norm_g, w_gate, w_up, w_down,
           final_norm_g):
    batch, seq, d = x.shape
    depth = ada_w.shape[0]
    ctx_len = ctx.shape[1]
    rows = seq // GRID_W
    n_ctx_rows = batch * ctx_len

    cc = jnp.zeros((16, d), F32).at[:batch].set(c).at[batch].set(c_ctx)
    mods = adaln(cc, ada_w, ada_b)

    def mod_vectors(l):
        mx = [mods[l, :batch, i * d:(i + 1) * d].reshape(batch, 1, d) for i in range(6)]
        mc = [mods[l, batch:batch + 1, i * d:(i + 1) * d].reshape(1, 1, d) for i in range(6)]
        return mx, mc

    dt_lo = SSD_WIDTH + SSD_XBC
    w_ssd = w_in[:, :, :dt_lo].astype(BF16)
    w_sc = w_in[:, :, SSD_COLS:].astype(BF16)
    w_dt = jnp.tile(w_in[:, :, dt_lo:SSD_COLS].astype(BF16), (1, 1, LANES // HEAD_DIRS))
    w_out_b = w_out.astype(BF16)
    w_down_b = w_down.astype(BF16)

    h_ctx = ctx.reshape(1, n_ctx_rows, d)
    mx, mc = mod_vectors(0)
    hx = norm_mod(x, mix_norm_g[0], mx[0], mx[1])
    hc = norm_mod(h_ctx, mix_norm_g[0], mc[0], mc[1])
    zero_state = jnp.zeros((batch, SSD_STATE, SSD_WIDTH), F32)
    zeros_d = jnp.zeros((batch, 1, d), F32)
    out = None

    for l in range(depth):
        last = l == depth - 1
        layer = (ssd_conv_w[l], ssd_conv_b[l], ssd_dt_bias[l], ssd_a_log[l], ssd_d[l],
                 ssd_norm_g[l], sc_conv_w[l])
        if not last:
            mx_next, mc_next = mod_vectors(l + 1)

        if last:
            _, _, state_f, state_b = _mixer(hc, batch, w_ssd, w_dt, None, l, layer,
                                            zero_state, zero_state)
        else:
            y_ssd, y_sc, state_f, state_b = _mixer(hc, batch, w_ssd, w_dt, w_sc, l, layer,
                                                   zero_state, zero_state)
            h_ctx, hf = out_proj(y_ssd.reshape(1, n_ctx_rows, -1), y_sc.reshape(1, n_ctx_rows, -1),
                                 w_out_b, l, h_ctx, mc[2], ffn_norm_g[l], mc[3], mc[4])
            act = ffn_up(hf, w_gate, w_up, l)
            h_ctx, hc = ffn_down(act, w_down_b, l, h_ctx, mc[5], mix_norm_g[l + 1],
                                 mc_next[0], mc_next[1], emit_x=True, h_dtype=BF16)

        col_major = l % 2 == 1
        if col_major:
            hx = _to_col_major(hx, rows)
        y_ssd, y_sc, _, _ = _mixer(hx, batch, w_ssd, w_dt, w_sc, l, layer, state_f, state_b)
        if col_major:
            y_ssd = _from_col_major(y_ssd, rows)
            y_sc = _from_col_major(y_sc, rows)
        x, hf = out_proj(y_ssd, y_sc, w_out_b, l, x, mx[2], ffn_norm_g[l], mx[3], mx[4])
        act = ffn_up(hf, w_gate, w_up, l)
        if last:
            (out,) = ffn_down(act, w_down_b, l, x, mx[5], final_norm_g, zeros_d, zeros_d,
                              emit_x=False, h_dtype=x.dtype)
        else:
            x, hx = ffn_down(act, w_down_b, l, x, mx[5], mix_norm_g[l + 1],
                             mx_next[0], mx_next[1], emit_x=True, h_dtype=BF16)
            mx, mc = mx_next, mc_next
    return out
```

```python
import functools

import jax
import jax.numpy as jnp
import numpy as np
from jax import lax
from jax.experimental import pallas as pl
from jax.experimental.pallas import tpu as pltpu

F32 = jnp.float32
BF16 = jnp.bfloat16

D_MODEL = 2048
GRID_W = 64
NORM_EPS = 1e-6
SSD_WIDTH = 1024
SSD_HEAD_DIM = 64
SSD_HEADS = 16
SSD_GROUPS = 2
SSD_STATE = 128
SSD_CONV_W = 5
SSD_CHUNK = 128
SSD_CHUNKS_PER_STEP = 8
SSD_GN = SSD_GROUPS * SSD_STATE
SSD_XBC = SSD_WIDTH + 2 * SSD_GN
SSD_COLS = SSD_WIDTH + SSD_XBC + 2 * SSD_HEADS
SC_WIDTH = 1024
SC_CONV_W = 3
FFN_HIDDEN = 5632
HEAD_GROUP_COLS = SSD_WIDTH // SSD_GROUPS
LOG2_E = 1.4426950408889634

LANES = 128
BF16_ROWS = 16
MXU_COLS = 256
MIB = 1024 * 1024
VMEM_LIMIT = 56 * MIB
FFN_DOWN_VMEM_LIMIT = 62 * MIB

ADALN_COLS = 1536
NORM_ROWS = 1024
PROJ_ROWS = 1024
OUT_PROJ_ROWS, OUT_PROJ_SUB_COLS = 512, 1024
FFN_UP_ROWS, FFN_UP_COLS, FFN_UP_SUB_ROWS = 2048, 512, 1024
FFN_DOWN_ROWS = 512
CONV_XBC_COLS, CONV_SC_COLS = 256, 512
CONV_ROWS = 256
CONV_PAD = BF16_ROWS
COND_ROWS = BF16_ROWS


def _params(*sem, vmem=VMEM_LIMIT):
    return pltpu.CompilerParams(dimension_semantics=sem, vmem_limit_bytes=vmem)


def _silu(v):
    return v * jax.nn.sigmoid(v)


def _softplus(v):
    return jnp.maximum(v, 0.0) + jnp.log1p(jnp.exp(-jnp.abs(v)))


def _norm_rows(v, g):
    ms = jnp.mean(v * v, axis=-1, keepdims=True)
    return v * lax.rsqrt(ms + NORM_EPS) * g


def _dot(a, b):
    return jnp.dot(a, b, preferred_element_type=F32)


def _adaln_kernel(c_ref, w_ref, b_ref, o_ref):
    s = _silu(c_ref[...]).astype(BF16)
    o_ref[0] = _dot(s, w_ref[0].astype(BF16)) + b_ref[0]


def adaln(cc, ada_w, ada_b, tn=ADALN_COLS):
    depth, d, n = ada_w.shape
    rows = cc.shape[0]
    return pl.pallas_call(
        _adaln_kernel,
        grid=(depth, n // tn),
        in_specs=[
            pl.BlockSpec((rows, d), lambda l, j: (0, 0)),
            pl.BlockSpec((1, d, tn), lambda l, j: (l, 0, j)),
            pl.BlockSpec((1, 1, tn), lambda l, j: (l, 0, j)),
        ],
        out_specs=pl.BlockSpec((1, rows, tn), lambda l, j: (l, 0, j)),
        out_shape=jax.ShapeDtypeStruct((depth, rows, n), F32),
        compiler_params=_params("arbitrary", "arbitrary"),
        name="adaln",
    )(cc, ada_w, ada_b.reshape(depth, 1, n))


def _norm_mod_kernel(x_ref, g_ref, sh_ref, sc_ref, o_ref):
    y = _norm_rows(x_ref[0], g_ref[...])
    o_ref[0] = (y * (1.0 + sc_ref[0]) + sh_ref[0]).astype(o_ref.dtype)


def norm_mod(x, g, shift, scale, tr=NORM_ROWS):
    nb, length, d = x.shape
    return pl.pallas_call(
        _norm_mod_kernel,
        grid=(nb, length // tr),
        in_specs=[
            pl.BlockSpec((1, tr, d), lambda b, i: (b, i, 0)),
            pl.BlockSpec((1, d), lambda b, i: (0, 0)),
            pl.BlockSpec((1, 1, d), lambda b, i: (b, 0, 0)),
            pl.BlockSpec((1, 1, d), lambda b, i: (b, 0, 0)),
        ],
        out_specs=pl.BlockSpec((1, tr, d), lambda b, i: (b, i, 0)),
        out_shape=jax.ShapeDtypeStruct((nb, length, d), BF16),
        compiler_params=_params("arbitrary", "arbitrary"),
        name="norm_mod",
    )(x, g.reshape(1, d), shift, scale)


def _matmul_kernel(a_ref, w_ref, o_ref, *, sub):
    a = a_ref[0]
    tn = o_ref.shape[-1]
    for n0 in range(0, tn, sub):
        o_ref[0, :, n0:n0 + sub] = _dot(a, w_ref[0, :, n0:n0 + sub]).astype(o_ref.dtype)


def matmul(a, w, layer, out_dtype, tm, tn, sub):
    nb, length, k = a.shape
    n = w.shape[-1]
    return pl.pallas_call(
        functools.partial(_matmul_kernel, sub=sub),
        grid=(n // tn, nb, length // tm),
        in_specs=[
            pl.BlockSpec((1, tm, k), lambda j, b, i: (b, i, 0)),
            pl.BlockSpec((1, k, tn), lambda j, b, i: (layer, 0, j)),
        ],
        out_specs=pl.BlockSpec((1, tm, tn), lambda j, b, i: (b, i, j)),
        out_shape=jax.ShapeDtypeStruct((nb, length, n), out_dtype),
        compiler_params=_params("arbitrary", "arbitrary", "arbitrary"),
        name="matmul",
    )(a, w)


def _in_proj_ssd_kernel(a_ref, w_ref, wdt_ref, o_ref, dt_ref, *, sub):
    a = a_ref[0]
    for n0 in range(0, o_ref.shape[-1], sub):
        o_ref[0, :, n0:n0 + sub] = _dot(a, w_ref[0, :, n0:n0 + sub]).astype(o_ref.dtype)
    dt_ref[0] = _dot(a, wdt_ref[0])


def in_proj_ssd(a, w, w_dt, layer, tm, sub=MXU_COLS):
    nb, length, k = a.shape
    n = w.shape[-1]
    tok = lambda b, i: (b, i, 0)
    of_layer = lambda b, i: (layer, 0, 0)
    return pl.pallas_call(
        functools.partial(_in_proj_ssd_kernel, sub=sub),
        grid=(nb, length // tm),
        in_specs=[pl.BlockSpec((1, tm, k), tok), pl.BlockSpec((1, k, n), of_layer),
                  pl.BlockSpec((1, k, LANES), of_layer)],
        out_specs=[pl.BlockSpec((1, tm, n), tok), pl.BlockSpec((1, tm, LANES), tok)],
        out_shape=[jax.ShapeDtypeStruct((nb, length, n), BF16), jax.ShapeDtypeStruct((nb, length, LANES), F32)],
        compiler_params=_params("arbitrary", "arbitrary"),
        name="in_proj_ssd",
    )(a, w, w_dt)


def _shifted_taps(pad_ref, r0, rows, taps):
    half = taps // 2
    span = rows + 2 * CONV_PAD
    win = pad_ref[r0:r0 + span, :]
    out = []
    for k in range(taps):
        shift = (half - k) % span
        rolled = win if shift == 0 else pltpu.roll(win, shift, axis=0)
        out.append(rolled[CONV_PAD:CONV_PAD + rows, :])
    return out


def _fill_padded(pad_ref, vals):
    t = vals.shape[0]
    zeros = jnp.zeros((CONV_PAD, vals.shape[1]), F32)
    pad_ref[0:CONV_PAD, :] = zeros
    pad_ref[t + CONV_PAD:t + 2 * CONV_PAD, :] = zeros
    pad_ref[CONV_PAD:t + CONV_PAD, :] = vals


def _conv_xbc_kernel(x_ref, w_ref, b_ref, o_ref, pad_ref, *, rows):
    t = x_ref.shape[1]
    _fill_padded(pad_ref, x_ref[0].astype(F32))
    for r0 in range(0, t, rows):
        taps = _shifted_taps(pad_ref, r0, rows, SSD_CONV_W)
        acc = b_ref[...] + w_ref[0:1, :] * taps[0]
        for k in range(1, SSD_CONV_W):
            acc = acc + w_ref[k:k + 1, :] * taps[k]
        o_ref[0, r0:r0 + rows, :] = _silu(acc).astype(o_ref.dtype)


def conv_xbc(p, conv_w, conv_b, ct=CONV_XBC_COLS):
    nseq, t, _ = p.shape
    off = SSD_WIDTH // ct
    w8 = jnp.zeros((8, SSD_XBC), F32).at[:SSD_CONV_W].set(conv_w)
    return pl.pallas_call(
        functools.partial(_conv_xbc_kernel, rows=min(t, CONV_ROWS)),
        grid=(nseq, SSD_XBC // ct),
        in_specs=[
            pl.BlockSpec((1, t, ct), lambda s, j: (s, 0, off + j)),
            pl.BlockSpec((8, ct), lambda s, j: (0, j)),
            pl.BlockSpec((1, ct), lambda s, j: (0, j)),
        ],
        out_specs=pl.BlockSpec((1, t, ct), lambda s, j: (s, 0, j)),
        out_shape=jax.ShapeDtypeStruct((nseq, t, SSD_XBC), BF16),
        scratch_shapes=[pltpu.VMEM((t + 2 * CONV_PAD, ct), F32)],
        compiler_params=_params("arbitrary", "arbitrary"),
        name="conv_xbc",
    )(p, w8, conv_b.reshape(1, SSD_XBC))


def _conv_sc_kernel(gb_ref, gc_ref, v_ref, w_ref, o_ref, pad_ref, *, rows):
    t = v_ref.shape[1]
    _fill_padded(pad_ref, gc_ref[0].astype(F32) * v_ref[0].astype(F32))
    for r0 in range(0, t, rows):
        taps = _shifted_taps(pad_ref, r0, rows, SC_CONV_W)
        acc = w_ref[0:1, :] * taps[0]
        for k in range(1, SC_CONV_W):
            acc = acc + w_ref[k:k + 1, :] * taps[k]
        gate = gb_ref[0, r0:r0 + rows, :].astype(F32)
        o_ref[0, r0:r0 + rows, :] = (gate * acc).astype(o_ref.dtype)


def conv_sc(p, sc_w, ct=CONV_SC_COLS):
    nseq, t, _ = p.shape
    base = 0
    step = SC_WIDTH // ct
    w8 = jnp.zeros((8, SC_WIDTH), F32).at[:SC_CONV_W].set(sc_w)
    col = lambda k: pl.BlockSpec((1, t, ct), lambda s, j: (s, 0, base + k * step + j))
    return pl.pallas_call(
        functools.partial(_conv_sc_kernel, rows=min(t, CONV_ROWS)),
        grid=(nseq, SC_WIDTH // ct),
        in_specs=[col(0), col(1), col(2), pl.BlockSpec((8, ct), lambda s, j: (0, j))],
        out_specs=pl.BlockSpec((1, t, ct), lambda s, j: (s, 0, j)),
        out_shape=jax.ShapeDtypeStruct((nseq, t, SC_WIDTH), BF16),
        scratch_shapes=[pltpu.VMEM((t + 2 * CONV_PAD, ct), F32)],
        compiler_params=_params("arbitrary", "arbitrary"),
        name="conv_sc",
    )(p, p, p, w8)


def _split3(v):
    hi = v.astype(BF16)
    r1 = v - hi.astype(F32)
    mid = r1.astype(BF16)
    return hi, mid, (r1 - mid.astype(F32)).astype(BF16)


def _cumsums(v, tri):
    return _dot(tri, jnp.concatenate(_split3(v), axis=0))


HEAD_DIRS = 2 * SSD_HEADS
SEL_ROWS = 2 * LANES
SEL_SPREAD = HEAD_DIRS * LANES
SEL_FWD = SEL_SPREAD + SSD_WIDTH
SEL_COLS = SEL_FWD + 2 * SSD_WIDTH


def _selection_matrix():
    s = np.zeros((SEL_ROWS, SEL_COLS), np.float32)
    for band in range(3):
        base = band * 2 * HEAD_DIRS
        for k in range(HEAD_DIRS):
            s[base + k, k * LANES:(k + 1) * LANES] = 1.0
        for h in range(SSD_HEADS):
            ch = slice(h * SSD_HEAD_DIM, (h + 1) * SSD_HEAD_DIM)
            s[base + HEAD_DIRS + h, SEL_SPREAD:SEL_FWD][ch] = 1.0
            s[base + SSD_HEADS + h, SEL_FWD:SEL_FWD + SSD_WIDTH][ch] = 1.0
            s[base + HEAD_DIRS + SSD_HEADS + h, SEL_FWD + SSD_WIDTH:SEL_COLS][ch] = 1.0
    return jnp.asarray(s, BF16)


def _packed_operand(acs, dt):
    lane = lax.broadcasted_iota(jnp.int32, acs.shape, 1)
    packed = jnp.where((lane & (2 * HEAD_DIRS - 1)) < HEAD_DIRS, acs, dt)
    hi, mid, lo = _split3(packed)
    first = jnp.where(lane < 2 * HEAD_DIRS, hi.astype(F32), mid.astype(F32)).astype(BF16)
    return jnp.concatenate([first, lo], axis=1)


def _state_update(st_ref, b_mat, xw, decay_row):
    for g in range(SSD_GROUPS):
        cols = slice(g * HEAD_GROUP_COLS, (g + 1) * HEAD_GROUP_COLS)
        contrib = lax.dot_general(
            b_mat[:, g * SSD_STATE:(g + 1) * SSD_STATE], xw[:, cols],
            (((0,), (0,)), ((), ())), preferred_element_type=F32)
        st_ref[:, cols] = st_ref[:, cols] * decay_row[:, cols] + contrib


def _ssd_kernel(xbc_ref, dt_ref, z_ref, h0f_ref, h0b_ref, dtb_ref, alog_ref, dsk_ref, g_ref, tri_ref,
                sel_ref, *refs, ng, per_step, emit_y):
    y_ref = refs[0] if emit_y else None
    hfl_ref, hbl_ref, hbe_ref, stf_ref, stb_ref = refs[-5:]
    q = SSD_CHUNK
    j = pl.program_id(1)
    fwd_lanes = 0
    bwd_lanes = SSD_HEADS

    def chunk_inputs(rows):
        dt = _softplus(dt_ref[0, rows, :] + dtb_ref[...])
        d_a = dt * (-jnp.exp(alog_ref[...]) * LOG2_E)
        xs = xbc_ref[0, rows, 0:SSD_WIDTH].astype(F32)
        b_mat = xbc_ref[0, rows, SSD_WIDTH:SSD_WIDTH + SSD_GN]
        return dt, d_a, xs, b_mat

    @pl.when(j == 0)
    def _():
        stf_ref[...] = h0f_ref[0]
        stb_ref[...] = h0b_ref[0]

    def backward_chunk(rows, c):
        dt, d_a, xs, b_mat = chunk_inputs(rows)
        acs_b = _cumsums(d_a, tri_ref[q:2 * q, :])
        spread = _dot(_packed_operand(acs_b, dt), sel_ref[:, SEL_FWD:SEL_COLS])
        sum_b = spread[:, 0:SSD_WIDTH]
        tot = sum_b[0:1, :]
        w_full = spread[:, SSD_WIDTH:2 * SSD_WIDTH] * jnp.exp2(tot - sum_b)
        xw = (xs * w_full).astype(BF16)
        if emit_y:
            hbe_ref[c] = stb_ref[...].astype(BF16)
        _state_update(stb_ref, b_mat, xw, jnp.exp2(tot))

    @pl.when(j < ng)
    def _():
        for sub in reversed(range(per_step)):
            backward_chunk(slice(sub * q, (sub + 1) * q), (ng - 1 - j) * per_step + sub)

        @pl.when(j == ng - 1)
        def _():
            hbl_ref[0] = stb_ref[...]

    def forward_chunk(rows, c):
        dt, d_a, xs, b_mat = chunk_inputs(rows)
        row = lax.broadcasted_iota(jnp.int32, (q, q), 0)
        lane = lax.broadcasted_iota(jnp.int32, (q, q), 1)
        sums = _cumsums(d_a, tri_ref[...])
        acs = jnp.where((lane & (HEAD_DIRS - 1)) < bwd_lanes, sums[0:q], sums[q:2 * q])
        spread = _dot(_packed_operand(acs, dt), sel_ref[:, 0:SEL_FWD])
        half = (lax.broadcasted_iota(jnp.int32, (q, SSD_WIDTH), 1) & (LANES - 1)) < SSD_HEAD_DIM

        def per_channel(first_block):
            even = [spread[:, (first_block + 2 * i) * LANES:(first_block + 2 * i + 1) * LANES]
                    for i in range(SSD_HEADS // 2)]
            odd = [spread[:, (first_block + 2 * i + 1) * LANES:(first_block + 2 * i + 2) * LANES]
                   for i in range(SSD_HEADS // 2)]
            return jnp.where(half, jnp.concatenate(even, axis=1), jnp.concatenate(odd, axis=1))

        sum_f = per_channel(fwd_lanes)
        e_f = jnp.exp2(sum_f)

        def leave_chunk():
            w_full = spread[:, SEL_SPREAD:SEL_FWD] * jnp.exp2(sum_f[q - 1:q, :] - sum_f)
            _state_update(stf_ref, b_mat, (xs * w_full).astype(BF16), e_f[q - 1:q, :])

        if not emit_y:
            leave_chunk()
            return

        acs_t = acs.T
        dt_t = dt.T
        src_t = acs_t - jnp.log2(dt_t)
        both_t = jnp.log2(dt_t[fwd_lanes:fwd_lanes + SSD_HEADS] + dt_t[bwd_lanes:bwd_lanes + SSD_HEADS])
        c_mat = xbc_ref[0, rows, SSD_WIDTH + SSD_GN:SSD_XBC]
        below = row > lane
        diag = row == lane

        cb = []
        for g in range(SSD_GROUPS):
            gs = slice(g * SSD_STATE, (g + 1) * SSD_STATE)
            cb.append(lax.dot_general(c_mat[:, gs], b_mat[:, gs], (((1,), (1,)), ((), ())),
                                      preferred_element_type=F32))

        y_parts = []
        for i in range(SSD_HEADS // 2):
            mats = []
            for h in (2 * i, 2 * i + 1):
                hf, hb = fwd_lanes + h, bwd_lanes + h
                tgt = jnp.where(below, spread[:, hf * LANES:(hf + 1) * LANES],
                                spread[:, hb * LANES:(hb + 1) * LANES])
                src = jnp.where(below, src_t[hf:hf + 1, :], src_t[hb:hb + 1, :])
                arg = jnp.where(diag, both_t[h:h + 1, :], tgt - src)
                mats.append((cb[h // (SSD_HEADS // SSD_GROUPS)] * jnp.exp2(arg)).astype(BF16))
            pair = xs[:, i * LANES:(i + 1) * LANES]
            top = jnp.where(lane < SSD_HEAD_DIM, pair, 0.0).astype(BF16)
            bot = jnp.where(lane >= SSD_HEAD_DIM, pair, 0.0).astype(BF16)
            y_parts.append(_dot(jnp.concatenate(mats, axis=1), jnp.concatenate([top, bot], axis=0)))
        y = jnp.concatenate(y_parts, axis=1)

        e_b = jnp.exp2(per_channel(bwd_lanes))
        st_f = stf_ref[...].astype(BF16)
        st_b = hbe_ref[c]
        off_f, off_b = [], []
        for g in range(SSD_GROUPS):
            gs = slice(g * SSD_STATE, (g + 1) * SSD_STATE)
            cols = slice(g * HEAD_GROUP_COLS, (g + 1) * HEAD_GROUP_COLS)
            off_f.append(_dot(c_mat[:, gs], st_f[:, cols]))
            off_b.append(_dot(c_mat[:, gs], st_b[:, cols]))
        y = y + e_f * jnp.concatenate(off_f, axis=1) + e_b * jnp.concatenate(off_b, axis=1)
        y = y + dsk_ref[...] * xs
        leave_chunk()

        z = z_ref[0, rows, :].astype(F32)
        y_ref[0, rows, :] = _norm_rows(y * _silu(z), g_ref[...]).astype(y_ref.dtype)

    @pl.when(j >= ng)
    def _():
        for sub in range(per_step):
            forward_chunk(slice(sub * q, (sub + 1) * q), (j - ng) * per_step + sub)

        @pl.when(j == 2 * ng - 1)
        def _():
            hfl_ref[0] = stf_ref[...]


def ssd(xbc, dt_raw, p, h0_f, h0_b, dt_bias, a_log, d_skip, norm_g, emit_y):
    nseq, t, _ = xbc.shape
    nc = t // SSD_CHUNK
    per_step = min(SSD_CHUNKS_PER_STEP, nc)
    ng = nc // per_step
    rows = per_step * SSD_CHUNK
    pad_row = lambda v: jnp.tile(v.reshape(1, HEAD_DIRS), (1, LANES // HEAD_DIRS))
    ones = np.ones((SSD_CHUNK, SSD_CHUNK), np.float32)
    tri = jnp.asarray(np.concatenate([np.tile(np.tril(ones), (1, 3)), np.tile(np.triu(ones), (1, 3))]), BF16)
    chunk = lambda s, j: (s, jnp.where(j < ng, ng - 1 - j, j - ng), 0)
    out_chunk = lambda s, j: (s, jnp.maximum(j - ng, 0), 0)
    const2 = lambda s, j: (0, 0)
    seq = lambda s, j: (s, 0, 0)
    state_shape = jax.ShapeDtypeStruct((nseq, SSD_STATE, SSD_WIDTH), F32)
    return pl.pallas_call(
        functools.partial(_ssd_kernel, ng=ng, per_step=per_step, emit_y=emit_y),
        grid=(nseq, 2 * ng),
        in_specs=[
            pl.BlockSpec((1, rows, SSD_XBC), chunk),
            pl.BlockSpec((1, rows, LANES), chunk),
            pl.BlockSpec((1, rows, SSD_WIDTH), out_chunk),
            pl.BlockSpec((1, SSD_STATE, SSD_WIDTH), seq),
            pl.BlockSpec((1, SSD_STATE, SSD_WIDTH), seq),
            pl.BlockSpec((1, LANES), const2),
            pl.BlockSpec((1, LANES), const2),
            pl.BlockSpec((1, SSD_WIDTH), const2),
            pl.BlockSpec((1, SSD_WIDTH), const2),
            pl.BlockSpec((2 * SSD_CHUNK, 3 * SSD_CHUNK), const2),
            pl.BlockSpec((SEL_ROWS, SEL_COLS), const2),
        ],
        out_specs=([pl.BlockSpec((1, rows, SSD_WIDTH), out_chunk)] if emit_y else [])
        + [pl.BlockSpec((1, SSD_STATE, SSD_WIDTH), seq)] * 2,
        out_shape=([jax.ShapeDtypeStruct((nseq, t, SSD_WIDTH), BF16)] if emit_y else []) + [state_shape] * 2,
        scratch_shapes=[
            pltpu.VMEM((nc, SSD_STATE, SSD_WIDTH), BF16),
            pltpu.VMEM((SSD_STATE, SSD_WIDTH), F32),
            pltpu.VMEM((SSD_STATE, SSD_WIDTH), F32),
        ],
        compiler_params=_params("arbitrary", "arbitrary"),
        name="ssd",
    )(xbc, dt_raw, p, h0_f, h0_b, pad_row(dt_bias), pad_row(a_log),
      jnp.repeat(d_skip, SSD_HEAD_DIM).reshape(1, SSD_WIDTH), norm_g.reshape(1, SSD_WIDTH), tri,
      _selection_matrix())


def _out_proj_kernel(y1_ref, y2_ref, w_ref, x_ref, gate_ref, g_ref, sh_ref, sc_ref,
                     xo_ref, h_ref, *, sub):
    y = jnp.concatenate([y1_ref[0], y2_ref[0]], axis=1)
    n = xo_ref.shape[-1]
    for n0 in range(0, n, sub):
        ns = slice(n0, n0 + sub)
        xo_ref[0, :, ns] = x_ref[0, :, ns] + gate_ref[0, :, ns] * _dot(y, w_ref[0, :, ns])
    hn = _norm_rows(xo_ref[0], g_ref[...])
    h_ref[0] = (hn * (1.0 + sc_ref[0]) + sh_ref[0]).astype(h_ref.dtype)


def out_proj(y_ssd, y_sc, w_out, layer, x, gate, g, shift, scale, tm=OUT_PROJ_ROWS, sub=OUT_PROJ_SUB_COLS):
    nb, length, d = x.shape
    k1, k2 = y_ssd.shape[-1], y_sc.shape[-1]
    tok = lambda b, i: (b, i, 0)
    per_b = lambda b, i: (b, 0, 0)
    return pl.pallas_call(
        functools.partial(_out_proj_kernel, sub=sub),
        grid=(nb, length // tm),
        in_specs=[
            pl.BlockSpec((1, tm, k1), tok),
            pl.BlockSpec((1, tm, k2), tok),
            pl.BlockSpec((1, k1 + k2, d), lambda b, i: (layer, 0, 0)),
            pl.BlockSpec((1, tm, d), tok),
            pl.BlockSpec((1, 1, d), per_b),
            pl.BlockSpec((1, d), lambda b, i: (0, 0)),
            pl.BlockSpec((1, 1, d), per_b),
            pl.BlockSpec((1, 1, d), per_b),
        ],
        out_specs=[pl.BlockSpec((1, tm, d), tok), pl.BlockSpec((1, tm, d), tok)],
        out_shape=[jax.ShapeDtypeStruct((nb, length, d), F32),
                   jax.ShapeDtypeStruct((nb, length, d), BF16)],
        compiler_params=_params("arbitrary", "arbitrary"),
        name="out_proj",
    )(y_ssd, y_sc, w_out, x, gate, g.reshape(1, d), shift, scale)


def _ffn_up_kernel(h_ref, wg_ref, wu_ref, o_ref, wgb_ref, wub_ref, *, sub_m):
    @pl.when((pl.program_id(1) == 0) & (pl.program_id(2) == 0))
    def _():
        wgb_ref[...] = wg_ref[0].astype(BF16)
        wub_ref[...] = wu_ref[0].astype(BF16)

    tm = h_ref.shape[1]
    for m0 in range(0, tm, sub_m):
        h = h_ref[0, m0:m0 + sub_m, :]
        o_ref[0, m0:m0 + sub_m, :] = (_silu(_dot(h, wgb_ref[...])) * _dot(h, wub_ref[...])).astype(o_ref.dtype)


def ffn_up(h, w_gate, w_up, layer, tm=FFN_UP_ROWS, tn=FFN_UP_COLS, sub_m=FFN_UP_SUB_ROWS):
    nb, length, d = h.shape
    n = w_gate.shape[-1]
    tm = min(tm, length)
    w_spec = pl.BlockSpec((1, d, tn), lambda j, b, i: (layer, 0, j))
    return pl.pallas_call(
        functools.partial(_ffn_up_kernel, sub_m=sub_m),
        grid=(n // tn, nb, length // tm),
        in_specs=[pl.BlockSpec((1, tm, d), lambda j, b, i: (b, i, 0)), w_spec, w_spec],
        out_specs=pl.BlockSpec((1, tm, tn), lambda j, b, i: (b, i, j)),
        out_shape=jax.ShapeDtypeStruct((nb, length, n), BF16),
        scratch_shapes=[pltpu.VMEM((d, tn), BF16), pltpu.VMEM((d, tn), BF16)],
        compiler_params=_params("arbitrary", "arbitrary", "arbitrary"),
        name="ffn_up",
    )(h, w_gate, w_up)


def _ffn_down_kernel(a_ref, w_ref, x_ref, gate_ref, g_ref, sh_ref, sc_ref, *refs, sub_n):
    res_ref, h_ref = refs[0], refs[-1]
    n = res_ref.shape[2]
    a = a_ref[0]
    for n0 in range(0, n, sub_n):
        ns = slice(n0, n0 + sub_n)
        res_ref[0, :, ns] = x_ref[0, :, ns] + gate_ref[0, :, ns] * _dot(a, w_ref[0, :, ns])
    hn = _norm_rows(res_ref[0], g_ref[...])
    h_ref[0] = (hn * (1.0 + sc_ref[0]) + sh_ref[0]).astype(h_ref.dtype)


def ffn_down(act, w_down, layer, x, gate, g, shift, scale, emit_x, h_dtype, tm=FFN_DOWN_ROWS):
    nb, length, d = x.shape
    kk = act.shape[-1]
    tok = lambda b, i: (b, i, 0)
    per_b = lambda b, i: (b, 0, 0)
    if emit_x:
        out_shape = [jax.ShapeDtypeStruct((nb, length, d), F32), jax.ShapeDtypeStruct((nb, length, d), h_dtype)]
    else:
        assert h_dtype == F32
        out_shape = [jax.ShapeDtypeStruct((nb, length, d), F32)]
    return pl.pallas_call(
        functools.partial(_ffn_down_kernel, sub_n=d // 2),
        grid=(nb, length // tm),
        in_specs=[
            pl.BlockSpec((1, tm, kk), tok),
            pl.BlockSpec((1, kk, d), lambda b, i: (layer, 0, 0), pipeline_mode=pl.Buffered(1)),
            pl.BlockSpec((1, tm, d), tok),
            pl.BlockSpec((1, 1, d), per_b),
            pl.BlockSpec((1, d), lambda b, i: (0, 0)),
            pl.BlockSpec((1, 1, d), per_b),
            pl.BlockSpec((1, 1, d), per_b),
        ],
        out_specs=[pl.BlockSpec((1, tm, d), tok)] * len(out_shape),
        out_shape=out_shape,
        compiler_params=_params("arbitrary", "arbitrary", vmem=FFN_DOWN_VMEM_LIMIT),
        name="ffn_down",
    )(act, w_down, x, gate, g.reshape(1, d), shift, scale)


def _to_col_major(t, rows):
    b, length, ch = t.shape
    return t.reshape(b, rows, GRID_W, ch).transpose(0, 2, 1, 3).reshape(b, length, ch)


def _from_col_major(t, rows):
    b, length, ch = t.shape
    return t.reshape(b, GRID_W, rows, ch).transpose(0, 2, 1, 3).reshape(b, length, ch)


def _mixer(h_rows, nseq, w_ssd, w_dt, w_sc, l, layer, h0_f, h0_b):
    (conv_w, conv_b, dt_bias, a_log, d_skip, ssd_g, sc_w) = layer
    nb, length, _ = h_rows.shape
    t = nb * length // nseq
    tm = min(length, PROJ_ROWS)
    p, dt_raw = in_proj_ssd(h_rows, w_ssd, w_dt, l, tm=tm)
    p = p.reshape(nseq, t, p.shape[-1])
    dt_raw = dt_raw.reshape(nseq, t, LANES)
    xbc = conv_xbc(p, conv_w, conv_b)
    *y_ssd, h_f, h_b = ssd(xbc, dt_raw, p, h0_f, h0_b, dt_bias, a_log, d_skip, ssd_g, emit_y=w_sc is not None)
    y_ssd = y_ssd[0] if y_ssd else None
    y_sc = None
    if w_sc is not None:
        n_sc = w_sc.shape[-1]
        p_sc = matmul(h_rows, w_sc, l, BF16, tm=tm, tn=n_sc, sub=MXU_COLS)
        y_sc = conv_sc(p_sc.reshape(nseq, t, n_sc), sc_w)
    return y_ssd, y_sc, h_f, h_b


def kernel(x, c, ctx, c_ctx, ada_w, ada_b, mix_norm_g, w_in, ssd_conv_w, ssd_conv_b, ssd_dt_bias,
           ssd_a_log, ssd_d, ssd_norm_g, sc_conv_w, w_out, ffn_norm_g, w_gate, w_up, w_down,
           final_norm_g):
    batch, seq, d = x.shape
    depth = ada_w.shape[0]
    ctx_len = ctx.shape[1]
    rows = seq // GRID_W
    n_ctx_rows = batch * ctx_len

    cc = jnp.zeros((COND_ROWS, d), F32).at[:batch].set(c).at[batch].set(c_ctx)
    mods = adaln(cc, ada_w, ada_b)

    def mod_vectors(l):
        mx = [mods[l, :batch, i * d:(i + 1) * d].reshape(batch, 1, d) for i in range(6)]
        mc = [mods[l, batch:batch + 1, i * d:(i + 1) * d].reshape(1, 1, d) for i in range(6)]
        return mx, mc

    dt_lo = SSD_WIDTH + SSD_XBC
    w_ssd = w_in[:, :, :dt_lo].astype(BF16)
    w_sc = w_in[:, :, SSD_COLS:].astype(BF16)
    w_dt = jnp.tile(w_in[:, :, dt_lo:SSD_COLS].astype(BF16), (1, 1, LANES // HEAD_DIRS))
    w_out_b = w_out.astype(BF16)
    w_down_b = w_down.astype(BF16)

    h_ctx = ctx.reshape(1, n_ctx_rows, d)
    mx, mc = mod_vectors(0)
    hx = norm_mod(x, mix_norm_g[0], mx[0], mx[1])
    hc = norm_mod(h_ctx, mix_norm_g[0], mc[0], mc[1])
    zero_state = jnp.zeros((batch, SSD_STATE, SSD_WIDTH), F32)
    zeros_d = jnp.zeros((batch, 1, d), F32)
    out = None

    for l in range(depth):
        last = l == depth - 1
        layer = (ssd_conv_w[l], ssd_conv_b[l], ssd_dt_bias[l], ssd_a_log[l], ssd_d[l],
                 ssd_norm_g[l], sc_conv_w[l])
        if not last:
            mx_next, mc_next = mod_vectors(l + 1)

        if last:
            _, _, state_f, state_b = _mixer(hc, batch, w_ssd, w_dt, None, l, layer,
                                            zero_state, zero_state)
        else:
            y_ssd, y_sc, state_f, state_b = _mixer(hc, batch, w_ssd, w_dt, w_sc, l, layer,
                                                   zero_state, zero_state)
            h_ctx, hf = out_proj(y_ssd.reshape(1, n_ctx_rows, -1), y_sc.reshape(1, n_ctx_rows, -1),
                                 w_out_b, l, h_ctx, mc[2], ffn_norm_g[l], mc[3], mc[4])
            act = ffn_up(hf, w_gate, w_up, l)
            h_ctx, hc = ffn_down(act, w_down_b, l, h_ctx, mc[5], mix_norm_g[l + 1],
                                 mc_next[0], mc_next[1], emit_x=True, h_dtype=BF16)

        col_major = l % 2 == 1
        if col_major:
            hx = _to_col_major(hx, rows)
        y_ssd, y_sc, _, _ = _mixer(hx, batch, w_ssd, w_dt, w_sc, l, layer, state_f, state_b)
        if col_major:
            y_ssd = _from_col_major(y_ssd, rows)
            y_sc = _from_col_major(y_sc, rows)
        x, hf = out_proj(y_ssd, y_sc, w_out_b, l, x, mx[2], ffn_norm_g[l], mx[3], mx[4])
        act = ffn_up(hf, w_gate, w_up, l)
        if last:
            (out,) = ffn_down(act, w_down_b, l, x, mx[5], final_norm_g, zeros_d, zeros_d,
                              emit_x=False, h_dtype=x.dtype)
        else:
            x, hx = ffn_down(act, w_down_b, l, x, mx[5], mix_norm_g[l + 1],
                             mx_next[0], mx_next[1], emit_x=True, h_dtype=BF16)
            mx, mc = mx_next, mc_next
    return out
```

```python
import functools

import jax
import jax.numpy as jnp
import numpy as np
from jax import lax
from jax.experimental import pallas as pl
from jax.experimental.pallas import tpu as pltpu

F32 = jnp.float32
BF16 = jnp.bfloat16

D_MODEL = 2048
GRID_W = 64
NORM_EPS = 1e-6
SSD_WIDTH = 1024
SSD_HEAD_DIM = 64
SSD_HEADS = 16
SSD_GROUPS = 2
SSD_STATE = 128
SSD_CONV_W = 5
SSD_CHUNK = 128
SSD_CHUNKS_PER_STEP = 8
SSD_GN = SSD_GROUPS * SSD_STATE
SSD_XBC = SSD_WIDTH + 2 * SSD_GN
SSD_COLS = SSD_WIDTH + SSD_XBC + 2 * SSD_HEADS
SC_WIDTH = 1024
SC_CONV_W = 3
FFN_HIDDEN = 5632
HEAD_GROUP_COLS = SSD_WIDTH // SSD_GROUPS
LOG2_E = 1.4426950408889634

LANES = 128
BF16_ROWS = 16
MXU_COLS = 256
MIB = 1024 * 1024
VMEM_LIMIT = 56 * MIB
FFN_DOWN_VMEM_LIMIT = 62 * MIB

ADALN_COLS = 1536
NORM_ROWS = 1024
PROJ_ROWS = 1024
OUT_PROJ_ROWS, OUT_PROJ_SUB_COLS = 512, 1024
FFN_UP_ROWS, FFN_UP_COLS, FFN_UP_SUB_ROWS = 2048, 512, 1024
FFN_DOWN_ROWS = 512
CONV_XBC_COLS, CONV_SC_COLS = 256, 512
CONV_ROWS = 256
CONV_PAD = BF16_ROWS
STREAM_BUFFERS = 3
COND_ROWS = BF16_ROWS


def _params(*sem, vmem=VMEM_LIMIT):
    return pltpu.CompilerParams(dimension_semantics=sem, vmem_limit_bytes=vmem)


def _silu(v):
    return v * jax.nn.sigmoid(v)


def _softplus(v):
    return jnp.maximum(v, 0.0) + jnp.log1p(jnp.exp(-jnp.abs(v)))


def _norm_rows(v, g):
    ms = jnp.mean(v * v, axis=-1, keepdims=True)
    return v * lax.rsqrt(ms + NORM_EPS) * g


def _dot(a, b):
    return jnp.dot(a, b, preferred_element_type=F32)


def _adaln_kernel(c_ref, w_ref, b_ref, o_ref):
    s = _silu(c_ref[...]).astype(BF16)
    o_ref[0] = _dot(s, w_ref[0].astype(BF16)) + b_ref[0]


def adaln(cc, ada_w, ada_b, tn=ADALN_COLS):
    depth, d, n = ada_w.shape
    rows = cc.shape[0]
    return pl.pallas_call(
        _adaln_kernel,
        grid=(depth, n // tn),
        in_specs=[
            pl.BlockSpec((rows, d), lambda l, j: (0, 0)),
            pl.BlockSpec((1, d, tn), lambda l, j: (l, 0, j)),
            pl.BlockSpec((1, 1, tn), lambda l, j: (l, 0, j)),
        ],
        out_specs=pl.BlockSpec((1, rows, tn), lambda l, j: (l, 0, j)),
        out_shape=jax.ShapeDtypeStruct((depth, rows, n), F32),
        compiler_params=_params("arbitrary", "arbitrary"),
        name="adaln",
    )(cc, ada_w, ada_b.reshape(depth, 1, n))


def _norm_mod_kernel(x_hbm, g_ref, sh_ref, sc_ref, o_ref, buf, sem, *, tr, tiles_per_batch, n_tiles):
    step = pl.program_id(0) * tiles_per_batch + pl.program_id(1)
    ahead = STREAM_BUFFERS - 1

    def fetch(s):
        slot = s % STREAM_BUFFERS
        src = x_hbm.at[s // tiles_per_batch, pl.ds((s % tiles_per_batch) * tr, tr), :]
        return pltpu.make_async_copy(src, buf.at[slot], sem.at[slot])

    @pl.when(step == 0)
    def _():
        for s in range(min(ahead, n_tiles)):
            fetch(s).start()

    @pl.when(step + ahead < n_tiles)
    def _():
        fetch(step + ahead).start()

    fetch(step).wait()
    y = _norm_rows(buf[step % STREAM_BUFFERS], g_ref[...])
    o_ref[0] = (y * (1.0 + sc_ref[0]) + sh_ref[0]).astype(o_ref.dtype)


def norm_mod(x, g, shift, scale, tr=NORM_ROWS):
    nb, length, d = x.shape
    tiles_per_batch = length // tr
    return pl.pallas_call(
        functools.partial(_norm_mod_kernel, tr=tr, tiles_per_batch=tiles_per_batch, n_tiles=nb * tiles_per_batch),
        grid=(nb, tiles_per_batch),
        in_specs=[
            pl.BlockSpec(memory_space=pl.ANY),
            pl.BlockSpec((1, d), lambda b, i: (0, 0)),
            pl.BlockSpec((1, 1, d), lambda b, i: (b, 0, 0)),
            pl.BlockSpec((1, 1, d), lambda b, i: (b, 0, 0)),
        ],
        out_specs=pl.BlockSpec((1, tr, d), lambda b, i: (b, i, 0)),
        out_shape=jax.ShapeDtypeStruct((nb, length, d), BF16),
        scratch_shapes=[pltpu.VMEM((STREAM_BUFFERS, tr, d), F32), pltpu.SemaphoreType.DMA((STREAM_BUFFERS,))],
        compiler_params=_params("arbitrary", "arbitrary"),
        name="norm_mod",
    )(x, g.reshape(1, d), shift, scale)


def _matmul_kernel(a_ref, w_ref, o_ref, *, sub):
    a = a_ref[0]
    tn = o_ref.shape[-1]
    for n0 in range(0, tn, sub):
        o_ref[0, :, n0:n0 + sub] = _dot(a, w_ref[0, :, n0:n0 + sub]).astype(o_ref.dtype)


def matmul(a, w, layer, out_dtype, tm, tn, sub):
    nb, length, k = a.shape
    n = w.shape[-1]
    return pl.pallas_call(
        functools.partial(_matmul_kernel, sub=sub),
        grid=(n // tn, nb, length // tm),
        in_specs=[
            pl.BlockSpec((1, tm, k), lambda j, b, i: (b, i, 0)),
            pl.BlockSpec((1, k, tn), lambda j, b, i: (layer, 0, j)),
        ],
        out_specs=pl.BlockSpec((1, tm, tn), lambda j, b, i: (b, i, j)),
        out_shape=jax.ShapeDtypeStruct((nb, length, n), out_dtype),
        compiler_params=_params("arbitrary", "arbitrary", "arbitrary"),
        name="matmul",
    )(a, w)


def _in_proj_ssd_kernel(a_ref, w_ref, wdt_ref, o_ref, dt_ref, *, sub):
    a = a_ref[0]
    for n0 in range(0, o_ref.shape[-1], sub):
        o_ref[0, :, n0:n0 + sub] = _dot(a, w_ref[0, :, n0:n0 + sub]).astype(o_ref.dtype)
    dt_ref[0] = _dot(a, wdt_ref[0])


def in_proj_ssd(a, w, w_dt, layer, tm, sub=MXU_COLS):
    nb, length, k = a.shape
    n = w.shape[-1]
    tok = lambda b, i: (b, i, 0)
    of_layer = lambda b, i: (layer, 0, 0)
    return pl.pallas_call(
        functools.partial(_in_proj_ssd_kernel, sub=sub),
        grid=(nb, length // tm),
        in_specs=[pl.BlockSpec((1, tm, k), tok), pl.BlockSpec((1, k, n), of_layer),
                  pl.BlockSpec((1, k, LANES), of_layer)],
        out_specs=[pl.BlockSpec((1, tm, n), tok), pl.BlockSpec((1, tm, LANES), tok)],
        out_shape=[jax.ShapeDtypeStruct((nb, length, n), BF16), jax.ShapeDtypeStruct((nb, length, LANES), F32)],
        compiler_params=_params("arbitrary", "arbitrary"),
        name="in_proj_ssd",
    )(a, w, w_dt)


def _shifted_taps(pad_ref, r0, rows, taps):
    half = taps // 2
    span = rows + 2 * CONV_PAD
    win = pad_ref[r0:r0 + span, :]
    out = []
    for k in range(taps):
        shift = (half - k) % span
        rolled = win if shift == 0 else pltpu.roll(win, shift, axis=0)
        out.append(rolled[CONV_PAD:CONV_PAD + rows, :])
    return out


def _fill_padded(pad_ref, vals):
    t = vals.shape[0]
    zeros = jnp.zeros((CONV_PAD, vals.shape[1]), F32)
    pad_ref[0:CONV_PAD, :] = zeros
    pad_ref[t + CONV_PAD:t + 2 * CONV_PAD, :] = zeros
    pad_ref[CONV_PAD:t + CONV_PAD, :] = vals


def _conv_xbc_kernel(x_ref, w_ref, b_ref, o_ref, pad_ref, *, rows):
    t = x_ref.shape[1]
    _fill_padded(pad_ref, x_ref[0].astype(F32))
    for r0 in range(0, t, rows):
        taps = _shifted_taps(pad_ref, r0, rows, SSD_CONV_W)
        acc = b_ref[...] + w_ref[0:1, :] * taps[0]
        for k in range(1, SSD_CONV_W):
            acc = acc + w_ref[k:k + 1, :] * taps[k]
        o_ref[0, r0:r0 + rows, :] = _silu(acc).astype(o_ref.dtype)


def conv_xbc(p, conv_w, conv_b, ct=CONV_XBC_COLS):
    nseq, t, _ = p.shape
    off = SSD_WIDTH // ct
    w8 = jnp.zeros((8, SSD_XBC), F32).at[:SSD_CONV_W].set(conv_w)
    return pl.pallas_call(
        functools.partial(_conv_xbc_kernel, rows=min(t, CONV_ROWS)),
        grid=(nseq, SSD_XBC // ct),
        in_specs=[
            pl.BlockSpec((1, t, ct), lambda s, j: (s, 0, off + j)),
            pl.BlockSpec((8, ct), lambda s, j: (0, j)),
            pl.BlockSpec((1, ct), lambda s, j: (0, j)),
        ],
        out_specs=pl.BlockSpec((1, t, ct), lambda s, j: (s, 0, j)),
        out_shape=jax.ShapeDtypeStruct((nseq, t, SSD_XBC), BF16),
        scratch_shapes=[pltpu.VMEM((t + 2 * CONV_PAD, ct), F32)],
        compiler_params=_params("arbitrary", "arbitrary"),
        name="conv_xbc",
    )(p, w8, conv_b.reshape(1, SSD_XBC))


def _conv_sc_kernel(gb_ref, gc_ref, v_ref, w_ref, o_ref, pad_ref, *, rows):
    t = v_ref.shape[1]
    _fill_padded(pad_ref, gc_ref[0].astype(F32) * v_ref[0].astype(F32))
    for r0 in range(0, t, rows):
        taps = _shifted_taps(pad_ref, r0, rows, SC_CONV_W)
        acc = w_ref[0:1, :] * taps[0]
        for k in range(1, SC_CONV_W):
            acc = acc + w_ref[k:k + 1, :] * taps[k]
        gate = gb_ref[0, r0:r0 + rows, :].astype(F32)
        o_ref[0, r0:r0 + rows, :] = (gate * acc).astype(o_ref.dtype)


def conv_sc(p, sc_w, ct=CONV_SC_COLS):
    nseq, t, _ = p.shape
    base = 0
    step = SC_WIDTH // ct
    w8 = jnp.zeros((8, SC_WIDTH), F32).at[:SC_CONV_W].set(sc_w)
    col = lambda k: pl.BlockSpec((1, t, ct), lambda s, j: (s, 0, base + k * step + j))
    return pl.pallas_call(
        functools.partial(_conv_sc_kernel, rows=min(t, CONV_ROWS)),
        grid=(nseq, SC_WIDTH // ct),
        in_specs=[col(0), col(1), col(2), pl.BlockSpec((8, ct), lambda s, j: (0, j))],
        out_specs=pl.BlockSpec((1, t, ct), lambda s, j: (s, 0, j)),
        out_shape=jax.ShapeDtypeStruct((nseq, t, SC_WIDTH), BF16),
        scratch_shapes=[pltpu.VMEM((t + 2 * CONV_PAD, ct), F32)],
        compiler_params=_params("arbitrary", "arbitrary"),
        name="conv_sc",
    )(p, p, p, w8)


def _split3(v):
    hi = v.astype(BF16)
    r1 = v - hi.astype(F32)
    mid = r1.astype(BF16)
    return hi, mid, (r1 - mid.astype(F32)).astype(BF16)


def _cumsums(v, tri):
    return _dot(tri, jnp.concatenate(_split3(v), axis=0))


HEAD_DIRS = 2 * SSD_HEADS
SEL_ROWS = 2 * LANES
SEL_SPREAD = HEAD_DIRS * LANES
SEL_FWD = SEL_SPREAD + SSD_WIDTH
SEL_COLS = SEL_FWD + 2 * SSD_WIDTH


def _selection_matrix():
    s = np.zeros((SEL_ROWS, SEL_COLS), np.float32)
    for band in range(3):
        base = band * 2 * HEAD_DIRS
        for k in range(HEAD_DIRS):
            s[base + k, k * LANES:(k + 1) * LANES] = 1.0
        for h in range(SSD_HEADS):
            ch = slice(h * SSD_HEAD_DIM, (h + 1) * SSD_HEAD_DIM)
            s[base + HEAD_DIRS + h, SEL_SPREAD:SEL_FWD][ch] = 1.0
            s[base + SSD_HEADS + h, SEL_FWD:SEL_FWD + SSD_WIDTH][ch] = 1.0
            s[base + HEAD_DIRS + SSD_HEADS + h, SEL_FWD + SSD_WIDTH:SEL_COLS][ch] = 1.0
    return jnp.asarray(s, BF16)


def _packed_operand(acs, dt):
    lane = lax.broadcasted_iota(jnp.int32, acs.shape, 1)
    packed = jnp.where((lane & (2 * HEAD_DIRS - 1)) < HEAD_DIRS, acs, dt)
    hi, mid, lo = _split3(packed)
    first = jnp.where(lane < 2 * HEAD_DIRS, hi.astype(F32), mid.astype(F32)).astype(BF16)
    return jnp.concatenate([first, lo], axis=1)


def _state_update(st_ref, b_mat, xw, decay_row):
    for g in range(SSD_GROUPS):
        cols = slice(g * HEAD_GROUP_COLS, (g + 1) * HEAD_GROUP_COLS)
        contrib = lax.dot_general(
            b_mat[:, g * SSD_STATE:(g + 1) * SSD_STATE], xw[:, cols],
            (((0,), (0,)), ((), ())), preferred_element_type=F32)
        st_ref[:, cols] = st_ref[:, cols] * decay_row[:, cols] + contrib


def _ssd_kernel(xbc_ref, dt_ref, z_ref, h0f_ref, h0b_ref, dtb_ref, alog_ref, dsk_ref, g_ref, tri_ref,
                sel_ref, *refs, ng, per_step, emit_y):
    y_ref = refs[0] if emit_y else None
    hfl_ref, hbl_ref, hbe_ref, stf_ref, stb_ref = refs[-5:]
    q = SSD_CHUNK
    j = pl.program_id(1)
    fwd_lanes = 0
    bwd_lanes = SSD_HEADS

    def chunk_inputs(rows):
        dt = _softplus(dt_ref[0, rows, :] + dtb_ref[...])
        d_a = dt * (-jnp.exp(alog_ref[...]) * LOG2_E)
        xs = xbc_ref[0, rows, 0:SSD_WIDTH].astype(F32)
        b_mat = xbc_ref[0, rows, SSD_WIDTH:SSD_WIDTH + SSD_GN]
        return dt, d_a, xs, b_mat

    @pl.when(j == 0)
    def _():
        stf_ref[...] = h0f_ref[0]
        stb_ref[...] = h0b_ref[0]

    def backward_chunk(rows, c):
        dt, d_a, xs, b_mat = chunk_inputs(rows)
        acs_b = _cumsums(d_a, tri_ref[q:2 * q, :])
        spread = _dot(_packed_operand(acs_b, dt), sel_ref[:, SEL_FWD:SEL_COLS])
        sum_b = spread[:, 0:SSD_WIDTH]
        tot = sum_b[0:1, :]
        w_full = spread[:, SSD_WIDTH:2 * SSD_WIDTH] * jnp.exp2(tot - sum_b)
        xw = (xs * w_full).astype(BF16)
        if emit_y:
            hbe_ref[c] = stb_ref[...].astype(BF16)
        _state_update(stb_ref, b_mat, xw, jnp.exp2(tot))

    @pl.when(j < ng)
    def _():
        for sub in reversed(range(per_step)):
            backward_chunk(slice(sub * q, (sub + 1) * q), (ng - 1 - j) * per_step + sub)

        @pl.when(j == ng - 1)
        def _():
            hbl_ref[0] = stb_ref[...]

    def forward_chunk(rows, c):
        dt, d_a, xs, b_mat = chunk_inputs(rows)
        row = lax.broadcasted_iota(jnp.int32, (q, q), 0)
        lane = lax.broadcasted_iota(jnp.int32, (q, q), 1)
        sums = _cumsums(d_a, tri_ref[...])
        acs = jnp.where((lane & (HEAD_DIRS - 1)) < bwd_lanes, sums[0:q], sums[q:2 * q])
        spread = _dot(_packed_operand(acs, dt), sel_ref[:, 0:SEL_FWD])
        half = (lax.broadcasted_iota(jnp.int32, (q, SSD_WIDTH), 1) & (LANES - 1)) < SSD_HEAD_DIM

        def per_channel(first_block):
            even = [spread[:, (first_block + 2 * i) * LANES:(first_block + 2 * i + 1) * LANES]
                    for i in range(SSD_HEADS // 2)]
            odd = [spread[:, (first_block + 2 * i + 1) * LANES:(first_block + 2 * i + 2) * LANES]
                   for i in range(SSD_HEADS // 2)]
            return jnp.where(half, jnp.concatenate(even, axis=1), jnp.concatenate(odd, axis=1))

        sum_f = per_channel(fwd_lanes)
        e_f = jnp.exp2(sum_f)

        def leave_chunk():
            w_full = spread[:, SEL_SPREAD:SEL_FWD] * jnp.exp2(sum_f[q - 1:q, :] - sum_f)
            _state_update(stf_ref, b_mat, (xs * w_full).astype(BF16), e_f[q - 1:q, :])

        if not emit_y:
            leave_chunk()
            return

        acs_t = acs.T
        dt_t = dt.T
        src_t = acs_t - jnp.log2(dt_t)
        both_t = jnp.log2(dt_t[fwd_lanes:fwd_lanes + SSD_HEADS] + dt_t[bwd_lanes:bwd_lanes + SSD_HEADS])
        c_mat = xbc_ref[0, rows, SSD_WIDTH + SSD_GN:SSD_XBC]
        below = row > lane
        diag = row == lane

        cb = []
        for g in range(SSD_GROUPS):
            gs = slice(g * SSD_STATE, (g + 1) * SSD_STATE)
            cb.append(lax.dot_general(c_mat[:, gs], b_mat[:, gs], (((1,), (1,)), ((), ())),
                                      preferred_element_type=F32))

        y_parts = []
        for i in range(SSD_HEADS // 2):
            mats = []
            for h in (2 * i, 2 * i + 1):
                hf, hb = fwd_lanes + h, bwd_lanes + h
                tgt = jnp.where(below, spread[:, hf * LANES:(hf + 1) * LANES],
                                spread[:, hb * LANES:(hb + 1) * LANES])
                src = jnp.where(below, src_t[hf:hf + 1, :], src_t[hb:hb + 1, :])
                arg = jnp.where(diag, both_t[h:h + 1, :], tgt - src)
                mats.append((cb[h // (SSD_HEADS // SSD_GROUPS)] * jnp.exp2(arg)).astype(BF16))
            pair = xs[:, i * LANES:(i + 1) * LANES]
            top = jnp.where(lane < SSD_HEAD_DIM, pair, 0.0).astype(BF16)
            bot = jnp.where(lane >= SSD_HEAD_DIM, pair, 0.0).astype(BF16)
            y_parts.append(_dot(jnp.concatenate(mats, axis=1), jnp.concatenate([top, bot], axis=0)))
        y = jnp.concatenate(y_parts, axis=1)

        e_b = jnp.exp2(per_channel(bwd_lanes))
        st_f = stf_ref[...].astype(BF16)
        st_b = hbe_ref[c]
        off_f, off_b = [], []
        for g in range(SSD_GROUPS):
            gs = slice(g * SSD_STATE, (g + 1) * SSD_STATE)
            cols = slice(g * HEAD_GROUP_COLS, (g + 1) * HEAD_GROUP_COLS)
            off_f.append(_dot(c_mat[:, gs], st_f[:, cols]))
            off_b.append(_dot(c_mat[:, gs], st_b[:, cols]))
        y = y + e_f * jnp.concatenate(off_f, axis=1) + e_b * jnp.concatenate(off_b, axis=1)
        y = y + dsk_ref[...] * xs
        leave_chunk()

        z = z_ref[0, rows, :].astype(F32)
        y_ref[0, rows, :] = _norm_rows(y * _silu(z), g_ref[...]).astype(y_ref.dtype)

    @pl.when(j >= ng)
    def _():
        for sub in range(per_step):
            forward_chunk(slice(sub * q, (sub + 1) * q), (j - ng) * per_step + sub)

        @pl.when(j == 2 * ng - 1)
        def _():
            hfl_ref[0] = stf_ref[...]


def ssd(xbc, dt_raw, p, h0_f, h0_b, dt_bias, a_log, d_skip, norm_g, emit_y):
    nseq, t, _ = xbc.shape
    nc = t // SSD_CHUNK
    per_step = min(SSD_CHUNKS_PER_STEP, nc)
    ng = nc // per_step
    rows = per_step * SSD_CHUNK
    pad_row = lambda v: jnp.tile(v.reshape(1, HEAD_DIRS), (1, LANES // HEAD_DIRS))
    ones = np.ones((SSD_CHUNK, SSD_CHUNK), np.float32)
    tri = jnp.asarray(np.concatenate([np.tile(np.tril(ones), (1, 3)), np.tile(np.triu(ones), (1, 3))]), BF16)
    chunk = lambda s, j: (s, jnp.where(j < ng, ng - 1 - j, j - ng), 0)
    out_chunk = lambda s, j: (s, jnp.maximum(j - ng, 0), 0)
    const2 = lambda s, j: (0, 0)
    seq = lambda s, j: (s, 0, 0)
    state_shape = jax.ShapeDtypeStruct((nseq, SSD_STATE, SSD_WIDTH), F32)
    return pl.pallas_call(
        functools.partial(_ssd_kernel, ng=ng, per_step=per_step, emit_y=emit_y),
        grid=(nseq, 2 * ng),
        in_specs=[
            pl.BlockSpec((1, rows, SSD_XBC), chunk),
            pl.BlockSpec((1, rows, LANES), chunk),
            pl.BlockSpec((1, rows, SSD_WIDTH), out_chunk),
            pl.BlockSpec((1, SSD_STATE, SSD_WIDTH), seq),
            pl.BlockSpec((1, SSD_STATE, SSD_WIDTH), seq),
            pl.BlockSpec((1, LANES), const2),
            pl.BlockSpec((1, LANES), const2),
            pl.BlockSpec((1, SSD_WIDTH), const2),
            pl.BlockSpec((1, SSD_WIDTH), const2),
            pl.BlockSpec((2 * SSD_CHUNK, 3 * SSD_CHUNK), const2),
            pl.BlockSpec((SEL_ROWS, SEL_COLS), const2),
        ],
        out_specs=([pl.BlockSpec((1, rows, SSD_WIDTH), out_chunk)] if emit_y else [])
        + [pl.BlockSpec((1, SSD_STATE, SSD_WIDTH), seq)] * 2,
        out_shape=([jax.ShapeDtypeStruct((nseq, t, SSD_WIDTH), BF16)] if emit_y else []) + [state_shape] * 2,
        scratch_shapes=[
            pltpu.VMEM((nc, SSD_STATE, SSD_WIDTH), BF16),
            pltpu.VMEM((SSD_STATE, SSD_WIDTH), F32),
            pltpu.VMEM((SSD_STATE, SSD_WIDTH), F32),
        ],
        compiler_params=_params("arbitrary", "arbitrary"),
        name="ssd",
    )(xbc, dt_raw, p, h0_f, h0_b, pad_row(dt_bias), pad_row(a_log),
      jnp.repeat(d_skip, SSD_HEAD_DIM).reshape(1, SSD_WIDTH), norm_g.reshape(1, SSD_WIDTH), tri,
      _selection_matrix())


def _out_proj_kernel(y1_ref, y2_ref, w_ref, x_ref, gate_ref, g_ref, sh_ref, sc_ref,
                     xo_ref, h_ref, *, sub):
    y = jnp.concatenate([y1_ref[0], y2_ref[0]], axis=1)
    n = xo_ref.shape[-1]
    for n0 in range(0, n, sub):
        ns = slice(n0, n0 + sub)
        xo_ref[0, :, ns] = x_ref[0, :, ns] + gate_ref[0, :, ns] * _dot(y, w_ref[0, :, ns])
    hn = _norm_rows(xo_ref[0], g_ref[...])
    h_ref[0] = (hn * (1.0 + sc_ref[0]) + sh_ref[0]).astype(h_ref.dtype)


def out_proj(y_ssd, y_sc, w_out, layer, x, gate, g, shift, scale, tm=OUT_PROJ_ROWS, sub=OUT_PROJ_SUB_COLS):
    nb, length, d = x.shape
    k1, k2 = y_ssd.shape[-1], y_sc.shape[-1]
    tok = lambda b, i: (b, i, 0)
    per_b = lambda b, i: (b, 0, 0)
    return pl.pallas_call(
        functools.partial(_out_proj_kernel, sub=sub),
        grid=(nb, length // tm),
        in_specs=[
            pl.BlockSpec((1, tm, k1), tok),
            pl.BlockSpec((1, tm, k2), tok),
            pl.BlockSpec((1, k1 + k2, d), lambda b, i: (layer, 0, 0)),
            pl.BlockSpec((1, tm, d), tok),
            pl.BlockSpec((1, 1, d), per_b),
            pl.BlockSpec((1, d), lambda b, i: (0, 0)),
            pl.BlockSpec((1, 1, d), per_b),
            pl.BlockSpec((1, 1, d), per_b),
        ],
        out_specs=[pl.BlockSpec((1, tm, d), tok), pl.BlockSpec((1, tm, d), tok)],
        out_shape=[jax.ShapeDtypeStruct((nb, length, d), F32),
                   jax.ShapeDtypeStruct((nb, length, d), BF16)],
        compiler_params=_params("arbitrary", "arbitrary"),
        name="out_proj",
    )(y_ssd, y_sc, w_out, x, gate, g.reshape(1, d), shift, scale)


def _ffn_up_kernel(h_ref, wg_ref, wu_ref, o_ref, wgb_ref, wub_ref, *, sub_m):
    @pl.when((pl.program_id(1) == 0) & (pl.program_id(2) == 0))
    def _():
        wgb_ref[...] = wg_ref[0].astype(BF16)
        wub_ref[...] = wu_ref[0].astype(BF16)

    tm = h_ref.shape[1]
    for m0 in range(0, tm, sub_m):
        h = h_ref[0, m0:m0 + sub_m, :]
        o_ref[0, m0:m0 + sub_m, :] = (_silu(_dot(h, wgb_ref[...])) * _dot(h, wub_ref[...])).astype(o_ref.dtype)


def ffn_up(h, w_gate, w_up, layer, tm=FFN_UP_ROWS, tn=FFN_UP_COLS, sub_m=FFN_UP_SUB_ROWS):
    nb, length, d = h.shape
    n = w_gate.shape[-1]
    tm = min(tm, length)
    w_spec = pl.BlockSpec((1, d, tn), lambda j, b, i: (layer, 0, j))
    return pl.pallas_call(
        functools.partial(_ffn_up_kernel, sub_m=sub_m),
        grid=(n // tn, nb, length // tm),
        in_specs=[pl.BlockSpec((1, tm, d), lambda j, b, i: (b, i, 0)), w_spec, w_spec],
        out_specs=pl.BlockSpec((1, tm, tn), lambda j, b, i: (b, i, j)),
        out_shape=jax.ShapeDtypeStruct((nb, length, n), BF16),
        scratch_shapes=[pltpu.VMEM((d, tn), BF16), pltpu.VMEM((d, tn), BF16)],
        compiler_params=_params("arbitrary", "arbitrary", "arbitrary"),
        name="ffn_up",
    )(h, w_gate, w_up)


def _ffn_down_kernel(a_ref, w_ref, x_ref, gate_ref, g_ref, sh_ref, sc_ref, *refs, sub_n):
    res_ref, h_ref = refs[0], refs[-1]
    n = res_ref.shape[2]
    a = a_ref[0]
    for n0 in range(0, n, sub_n):
        ns = slice(n0, n0 + sub_n)
        res_ref[0, :, ns] = x_ref[0, :, ns] + gate_ref[0, :, ns] * _dot(a, w_ref[0, :, ns])
    hn = _norm_rows(res_ref[0], g_ref[...])
    h_ref[0] = (hn * (1.0 + sc_ref[0]) + sh_ref[0]).astype(h_ref.dtype)


def ffn_down(act, w_down, layer, x, gate, g, shift, scale, emit_x, h_dtype, tm=FFN_DOWN_ROWS):
    nb, length, d = x.shape
    kk = act.shape[-1]
    tok = lambda b, i: (b, i, 0)
    per_b = lambda b, i: (b, 0, 0)
    if emit_x:
        out_shape = [jax.ShapeDtypeStruct((nb, length, d), F32), jax.ShapeDtypeStruct((nb, length, d), h_dtype)]
    else:
        assert h_dtype == F32
        out_shape = [jax.ShapeDtypeStruct((nb, length, d), F32)]
    return pl.pallas_call(
        functools.partial(_ffn_down_kernel, sub_n=d // 2),
        grid=(nb, length // tm),
        in_specs=[
            pl.BlockSpec((1, tm, kk), tok),
            pl.BlockSpec((1, kk, d), lambda b, i: (layer, 0, 0), pipeline_mode=pl.Buffered(1)),
            pl.BlockSpec((1, tm, d), tok),
            pl.BlockSpec((1, 1, d), per_b),
            pl.BlockSpec((1, d), lambda b, i: (0, 0)),
            pl.BlockSpec((1, 1, d), per_b),
            pl.BlockSpec((1, 1, d), per_b),
        ],
        out_specs=[pl.BlockSpec((1, tm, d), tok)] * len(out_shape),
        out_shape=out_shape,
        compiler_params=_params("arbitrary", "arbitrary", vmem=FFN_DOWN_VMEM_LIMIT),
        name="ffn_down",
    )(act, w_down, x, gate, g.reshape(1, d), shift, scale)


def _to_col_major(t, rows):
    b, length, ch = t.shape
    return t.reshape(b, rows, GRID_W, ch).transpose(0, 2, 1, 3).reshape(b, length, ch)


def _from_col_major(t, rows):
    b, length, ch = t.shape
    return t.reshape(b, GRID_W, rows, ch).transpose(0, 2, 1, 3).reshape(b, length, ch)


def _mixer(h_rows, nseq, w_ssd, w_dt, w_sc, l, layer, h0_f, h0_b):
    (conv_w, conv_b, dt_bias, a_log, d_skip, ssd_g, sc_w) = layer
    nb, length, _ = h_rows.shape
    t = nb * length // nseq
    tm = min(length, PROJ_ROWS)
    p, dt_raw = in_proj_ssd(h_rows, w_ssd, w_dt, l, tm=tm)
    p = p.reshape(nseq, t, p.shape[-1])
    dt_raw = dt_raw.reshape(nseq, t, LANES)
    xbc = conv_xbc(p, conv_w, conv_b)
    *y_ssd, h_f, h_b = ssd(xbc, dt_raw, p, h0_f, h0_b, dt_bias, a_log, d_skip, ssd_g, emit_y=w_sc is not None)
    y_ssd = y_ssd[0] if y_ssd else None
    y_sc = None
    if w_sc is not None:
        n_sc = w_sc.shape[-1]
        p_sc = matmul(h_rows, w_sc, l, BF16, tm=tm, tn=n_sc, sub=MXU_COLS)
        y_sc = conv_sc(p_sc.reshape(nseq, t, n_sc), sc_w)
    return y_ssd, y_sc, h_f, h_b


def kernel(x, c, ctx, c_ctx, ada_w, ada_b, mix_norm_g, w_in, ssd_conv_w, ssd_conv_b, ssd_dt_bias,
           ssd_a_log, ssd_d, ssd_norm_g, sc_conv_w, w_out, ffn_norm_g, w_gate, w_up, w_down,
           final_norm_g):
    batch, seq, d = x.shape
    depth = ada_w.shape[0]
    ctx_len = ctx.shape[1]
    rows = seq // GRID_W
    n_ctx_rows = batch * ctx_len

    cc = jnp.zeros((COND_ROWS, d), F32).at[:batch].set(c).at[batch].set(c_ctx)
    mods = adaln(cc, ada_w, ada_b)

    def mod_vectors(l):
        mx = [mods[l, :batch, i * d:(i + 1) * d].reshape(batch, 1, d) for i in range(6)]
        mc = [mods[l, batch:batch + 1, i * d:(i + 1) * d].reshape(1, 1, d) for i in range(6)]
        return mx, mc

    dt_lo = SSD_WIDTH + SSD_XBC
    w_ssd = w_in[:, :, :dt_lo].astype(BF16)
    w_sc = w_in[:, :, SSD_COLS:].astype(BF16)
    w_dt = jnp.tile(w_in[:, :, dt_lo:SSD_COLS].astype(BF16), (1, 1, LANES // HEAD_DIRS))
    w_out_b = w_out.astype(BF16)
    w_down_b = w_down.astype(BF16)

    h_ctx = ctx.reshape(1, n_ctx_rows, d)
    mx, mc = mod_vectors(0)
    hx = norm_mod(x, mix_norm_g[0], mx[0], mx[1])
    hc = norm_mod(h_ctx, mix_norm_g[0], mc[0], mc[1])
    zero_state = jnp.zeros((batch, SSD_STATE, SSD_WIDTH), F32)
    zeros_d = jnp.zeros((batch, 1, d), F32)
    out = None

    for l in range(depth):
        last = l == depth - 1
        layer = (ssd_conv_w[l], ssd_conv_b[l], ssd_dt_bias[l], ssd_a_log[l], ssd_d[l],
                 ssd_norm_g[l], sc_conv_w[l])
        if not last:
            mx_next, mc_next = mod_vectors(l + 1)

        if last:
            _, _, state_f, state_b = _mixer(hc, batch, w_ssd, w_dt, None, l, layer,
                                            zero_state, zero_state)
        else:
            y_ssd, y_sc, state_f, state_b = _mixer(hc, batch, w_ssd, w_dt, w_sc, l, layer,
                                                   zero_state, zero_state)
            h_ctx, hf = out_proj(y_ssd.reshape(1, n_ctx_rows, -1), y_sc.reshape(1, n_ctx_rows, -1),
                                 w_out_b, l, h_ctx, mc[2], ffn_norm_g[l], mc[3], mc[4])
            act = ffn_up(hf, w_gate, w_up, l)
            h_ctx, hc = ffn_down(act, w_down_b, l, h_ctx, mc[5], mix_norm_g[l + 1],
                                 mc_next[0], mc_next[1], emit_x=True, h_dtype=BF16)

        col_major = l % 2 == 1
        if col_major:
            hx = _to_col_major(hx, rows)
        y_ssd, y_sc, _, _ = _mixer(hx, batch, w_ssd, w_dt, w_sc, l, layer, state_f, state_b)
        if col_major:
            y_ssd = _from_col_major(y_ssd, rows)
            y_sc = _from_col_major(y_sc, rows)
        x, hf = out_proj(y_ssd, y_sc, w_out_b, l, x, mx[2], ffn_norm_g[l], mx[3], mx[4])
        act = ffn_up(hf, w_gate, w_up, l)
        if last:
            (out,) = ffn_down(act, w_down_b, l, x, mx[5], final_norm_g, zeros_d, zeros_d,
                              emit_x=False, h_dtype=x.dtype)
        else:
            x, hx = ffn_down(act, w_down_b, l, x, mx[5], mix_norm_g[l + 1],
                             mx_next[0], mx_next[1], emit_x=True, h_dtype=BF16)
            mx, mc = mx_next, mc_next
    return out
```
